```python
import jax, jax.numpy as jnp
from jax import lax
import numpy as np


D_MODEL = 1024
BATCH = 8
SEQ = 2048
DEPTH = 4
DEC_BATCH = 2
DEC_SEQ = 8192
PAST_LEN = 128

GRID_W = 64
HEAD_DIM = 64
NA_HEADS = 8
NA_WIN_H = 8
NA_WIN_W = 16
NA_WIDTH = NA_HEADS * HEAD_DIM
GQA_HEADS = 8
GQA_KV_HEADS = 2
GQA_GROUP = GQA_HEADS // GQA_KV_HEADS
GQA_WIDTH = GQA_HEADS * HEAD_DIM
MLA_HEADS = 8
MLA_Q_LORA = 384
MLA_KV_LORA = 256
MLA_NOPE_DIM = 64
MLA_ROPE_DIM = 32
MLA_V_DIM = 64
MLA_WIDTH = MLA_HEADS * MLA_V_DIM
N_BRANCHES = 3
FFN_DIM = 2816
N_EXPERTS = 8
TOP_K = 2
EXPERT_FFN_DIM = 3584
N_DENSE = (DEPTH + 1) // 2
N_MOE = DEPTH // 2
Q_BLOCK = 128
ROPE_THETA = 10000.0
RMS_EPS = 1e-6
LN_EPS = 1e-5
NEG_INF = -1e30
DEEPNORM_ALPHA = (2 * DEPTH) ** 0.25
DEEPNORM_BETA = (8 * DEPTH) ** -0.25
IN_SPLITS = (NA_WIDTH, NA_WIDTH, NA_WIDTH,
             GQA_WIDTH, GQA_KV_HEADS * HEAD_DIM, GQA_KV_HEADS * HEAD_DIM,
             MLA_Q_LORA, MLA_KV_LORA, MLA_ROPE_DIM,
             N_BRANCHES * D_MODEL)
IN_COLS = sum(IN_SPLITS)

kernel_name = 'hybrid_natten_gqa_mla_encoder'


def _layernorm(x, g, b):
    xf = x.astype(jnp.float32)
    mu = jnp.mean(xf, -1, keepdims=True)
    var = jnp.mean(jnp.square(xf - mu), -1, keepdims=True)
    y = (xf - mu) * lax.rsqrt(var + LN_EPS) * g.astype(jnp.float32) + b.astype(jnp.float32)
    return y.astype(x.dtype)


def _rmsnorm(x, g):
    xf = x.astype(jnp.float32)
    y = xf * lax.rsqrt(jnp.mean(jnp.square(xf), -1, keepdims=True) + RMS_EPS) * g.astype(jnp.float32)
    return y.astype(x.dtype)


def _rope(x, pos):
    half = x.shape[-1] // 2
    inv_freq = ROPE_THETA ** (-jnp.arange(half, dtype=jnp.float32) / half)
    ang = pos.astype(jnp.float32)[:, None] * inv_freq[None, :]
    cos = jnp.cos(ang)[:, None, :]
    sin = jnp.sin(ang)[:, None, :]
    xf = x.astype(jnp.float32)
    x1, x2 = xf[..., :half], xf[..., half:]
    return jnp.concatenate([x1 * cos - x2 * sin, x2 * cos + x1 * sin], -1).astype(x.dtype)


def _axial_rope(x, row, col):
    half = x.shape[-1] // 2
    return jnp.concatenate([_rope(x[..., :half], row), _rope(x[..., half:], col)], -1)


def _neighbourhood_attention(q, k, v, rpb):
    b, s, h, hd = q.shape
    rows = s // GRID_W
    wh = min(NA_WIN_H, rows)
    ncb = GRID_W // NA_WIN_W
    kw = 2 * NA_WIN_W
    r = np.arange(rows)
    r0 = np.clip(r - wh // 2, 0, rows - wh)
    key_rows = r0[:, None] + np.arange(wh)[None, :]
    jb = np.arange(ncb)
    kb = np.clip(jb * NA_WIN_W - NA_WIN_W // 2, 0, GRID_W - kw)
    key_cols = kb[:, None] + np.arange(kw)[None, :]
    q_cols = jb[:, None] * NA_WIN_W + np.arange(NA_WIN_W)[None, :]
    c0 = np.clip(q_cols - NA_WIN_W // 2, 0, GRID_W - NA_WIN_W)
    col_ok = (key_cols[:, None, :] >= c0[..., None]) & (key_cols[:, None, :] < c0[..., None] + NA_WIN_W)
    dr_idx = key_rows - r[:, None] + (NA_WIN_H - 1)
    dc_idx = np.clip(key_cols[:, None, :] - q_cols[..., None] + (NA_WIN_W - 1), 0, 2 * NA_WIN_W - 2)
    bias = rpb[:, dr_idx[:, None, None, :, None], dc_idx[None, :, :, None, :]].astype(jnp.float32)
    bias = jnp.where(col_ok[None, None, :, :, None, :], bias, NEG_INF).reshape(h, rows, ncb, NA_WIN_W, wh * kw)
    qg = q.reshape(b, rows, ncb, NA_WIN_W, h, hd)
    k_idx_r = key_rows[:, None, :, None]
    k_idx_c = key_cols[None, :, None, :]
    kg = k.reshape(b, rows, GRID_W, h, hd)[:, k_idx_r, k_idx_c].reshape(b, rows, ncb, wh * kw, h, hd)
    vg = v.reshape(b, rows, GRID_W, h, hd)[:, k_idx_r, k_idx_c].reshape(b, rows, ncb, wh * kw, h, hd)
    sc = jnp.einsum('brjqhd,brjkhd->bhrjqk', qg, kg).astype(jnp.float32) * (hd ** -0.5) + bias[None]
    p = jax.nn.softmax(sc, axis=-1).astype(v.dtype)
    out = jnp.einsum('bhrjqk,brjkhd->brjqhd', p, vg)
    return out.reshape(b, s, h * hd)


def _block_attention(q, k, v, scale):
    b, s, n_kv, g, dq = q.shape
    n_blocks = s // Q_BLOCK
    q_blocks = jnp.moveaxis(q.reshape(b, n_blocks, Q_BLOCK, n_kv, g, dq), 1, 0)

    def attend(qb):
        sc = jnp.einsum('bqngd,bknd->bngqk', qb, k).astype(jnp.float32) * scale
        p = jax.nn.softmax(sc, axis=-1).astype(v.dtype)
        return jnp.einsum('bngqk,bknd->bqngd', p, v)

    out = lax.map(attend, q_blocks)
    return jnp.moveaxis(out, 0, 1).reshape(b, s, n_kv * g * v.shape[-1])


def _token_mixers(x, w_in, na_rpb, gqa_q_norm, gqa_k_norm, mla_q_norm, mla_w_uq, mla_kv_norm, mla_w_ukv,
                  w_branch_a, w_branch_b, w_branch_c, w_out):
    b, s, _ = x.shape
    pos = jnp.arange(s, dtype=jnp.int32)
    row, col = pos // GRID_W, pos % GRID_W
    h = x @ w_in
    split_points = [int(i) for i in np.cumsum(IN_SPLITS)[:-1]]
    na_q, na_k, na_v, g_q, g_k, g_v, c_q, c_kv, k_rope, gate_logits = jnp.split(h, split_points, axis=-1)
    y_a = _neighbourhood_attention(na_q.reshape(b, s, NA_HEADS, HEAD_DIM), na_k.reshape(b, s, NA_HEADS, HEAD_DIM),
                                   na_v.reshape(b, s, NA_HEADS, HEAD_DIM), na_rpb)
    q = _axial_rope(_rmsnorm(g_q.reshape(b, s, GQA_HEADS, HEAD_DIM), gqa_q_norm), row, col)
    k = _axial_rope(_rmsnorm(g_k.reshape(b, s, GQA_KV_HEADS, HEAD_DIM), gqa_k_norm), row, col)
    v = g_v.reshape(b, s, GQA_KV_HEADS, HEAD_DIM)
    y_b = _block_attention(q.reshape(b, s, GQA_KV_HEADS, GQA_GROUP, HEAD_DIM), k, v, HEAD_DIM ** -0.5)
    q_c = (_rmsnorm(c_q, mla_q_norm) @ mla_w_uq).reshape(b, s, MLA_HEADS, MLA_NOPE_DIM + MLA_ROPE_DIM)
    kv_c = (_rmsnorm(c_kv, mla_kv_norm) @ mla_w_ukv).reshape(b, s, MLA_HEADS, MLA_NOPE_DIM + MLA_V_DIM)
    q_c = jnp.concatenate([q_c[..., :MLA_NOPE_DIM], _rope(q_c[..., MLA_NOPE_DIM:], pos)], -1)
    k_r = _rope(k_rope[:, :, None, :], pos)
    k_c = jnp.concatenate([kv_c[..., :MLA_NOPE_DIM], jnp.broadcast_to(k_r, (b, s, MLA_HEADS, MLA_ROPE_DIM))], -1)
    v_c = kv_c[..., MLA_NOPE_DIM:]
    y_c = _block_attention(q_c[:, :, :, None, :], k_c, v_c, (MLA_NOPE_DIM + MLA_ROPE_DIM) ** -0.5)
    gates = jax.nn.sigmoid(gate_logits.astype(jnp.float32)).astype(x.dtype).reshape(b, s, N_BRANCHES, D_MODEL)
    merged = (gates[:, :, 0] * (y_a @ w_branch_a) + gates[:, :, 1] * (y_b @ w_branch_b)
              + gates[:, :, 2] * (y_c @ w_branch_c))
    return merged @ w_out


def _swiglu(x, w1, w3, w2):
    return (jax.nn.silu(x @ w1) * (x @ w3)) @ w2


def _moe(x, router, w1, w3, w2):
    logits = (x @ router).astype(jnp.float32)
    top_v, top_i = lax.top_k(logits, TOP_K)
    wts = jax.nn.softmax(top_v, axis=-1)
    gate = jnp.sum(jax.nn.one_hot(top_i, N_EXPERTS, dtype=jnp.float32) * wts[..., None], axis=-2)
    out = jnp.zeros(x.shape, jnp.float32)
    for e in range(N_EXPERTS):
        out = out + gate[..., e:e + 1] * _swiglu(x, w1[e], w3[e], w2[e]).astype(jnp.float32)
    return out.astype(x.dtype)


def _encode(x, w_in, na_rpb, gqa_q_norm, gqa_k_norm, mla_q_norm, mla_w_uq, mla_kv_norm, mla_w_ukv,
            w_branch_a, w_branch_b, w_branch_c, w_out, ln1_g, ln1_b, ln2_g, ln2_b,
            ffn_w1, ffn_w3, ffn_w2, moe_router, moe_w1, moe_w3, moe_w2):
    for i in range(DEPTH):
        mix = _token_mixers(x, w_in[i], na_rpb[i], gqa_q_norm[i], gqa_k_norm[i], mla_q_norm[i], mla_w_uq[i],
                            mla_kv_norm[i], mla_w_ukv[i], w_branch_a[i], w_branch_b[i], w_branch_c[i], w_out[i])
        x = _layernorm(DEEPNORM_ALPHA * x + mix, ln1_g[i], ln1_b[i])
        j = i // 2
        if i % 2 == 0:
            ff = _swiglu(x, ffn_w1[j], ffn_w3[j], ffn_w2[j])
        else:
            ff = _moe(x, moe_router[j], moe_w1[j], moe_w3[j], moe_w2[j])
        x = _layernorm(DEEPNORM_ALPHA * x + ff, ln2_g[i], ln2_b[i])
    return x


def setup_inputs(seed: int = 0) -> dict:
    key = jax.random.key(seed)
    ks = jax.random.split(key, 26)
    f32 = jnp.float32

    def nrm(k, shape, scale):
        return jax.random.normal(k, shape, f32) * scale

    def gain(k, shape):
        return 1.0 + 0.1 * jax.random.normal(k, shape, f32)

    return {
        'x_prompt': nrm(ks[0], (BATCH, SEQ, D_MODEL), 1.0),
        'x_sample': nrm(ks[1], (DEC_BATCH, DEC_SEQ, D_MODEL), 1.0),
        'w_in': nrm(ks[2], (DEPTH, D_MODEL, IN_COLS), D_MODEL ** -0.5),
        'na_rpb': nrm(ks[3], (DEPTH, NA_HEADS, 2 * NA_WIN_H - 1, 2 * NA_WIN_W - 1), 0.2),
        'gqa_q_norm': gain(ks[4], (DEPTH, HEAD_DIM)),
        'gqa_k_norm': gain(ks[5], (DEPTH, HEAD_DIM)),
        'mla_q_norm': gain(ks[6], (DEPTH, MLA_Q_LORA)),
        'mla_w_uq': nrm(ks[7], (DEPTH, MLA_Q_LORA, MLA_HEADS * (MLA_NOPE_DIM + MLA_ROPE_DIM)), MLA_Q_LORA ** -0.5),
        'mla_kv_norm': gain(ks[8], (DEPTH, MLA_KV_LORA)),
        'mla_w_ukv': nrm(ks[9], (DEPTH, MLA_KV_LORA, MLA_HEADS * (MLA_NOPE_DIM + MLA_V_DIM)), MLA_KV_LORA ** -0.5),
        'w_branch_a': nrm(ks[10], (DEPTH, NA_WIDTH, D_MODEL), NA_WIDTH ** -0.5),
        'w_branch_b': nrm(ks[11], (DEPTH, GQA_WIDTH, D_MODEL), GQA_WIDTH ** -0.5),
        'w_branch_c': nrm(ks[12], (DEPTH, MLA_WIDTH, D_MODEL), MLA_WIDTH ** -0.5),
        'w_out': nrm(ks[13], (DEPTH, D_MODEL, D_MODEL), DEEPNORM_BETA * D_MODEL ** -0.5),
        'ln1_g': gain(ks[14], (DEPTH, D_MODEL)),
        'ln1_b': nrm(ks[15], (DEPTH, D_MODEL), 0.02),
        'ln2_g': gain(ks[16], (DEPTH, D_MODEL)),
        'ln2_b': nrm(ks[17], (DEPTH, D_MODEL), 0.02),
        'ffn_w1': nrm(ks[18], (N_DENSE, D_MODEL, FFN_DIM), D_MODEL ** -0.5),
        'ffn_w3': nrm(ks[19], (N_DENSE, D_MODEL, FFN_DIM), D_MODEL ** -0.5),
        'ffn_w2': nrm(ks[20], (N_DENSE, FFN_DIM, D_MODEL), DEEPNORM_BETA * FFN_DIM ** -0.5),
        'moe_router': nrm(ks[21], (N_MOE, D_MODEL, N_EXPERTS), D_MODEL ** -0.5),
        'moe_w1': nrm(ks[22], (N_MOE, N_EXPERTS, D_MODEL, EXPERT_FFN_DIM), D_MODEL ** -0.5),
        'moe_w3': nrm(ks[23], (N_MOE, N_EXPERTS, D_MODEL, EXPERT_FFN_DIM), D_MODEL ** -0.5),
        'moe_w2': nrm(ks[24], (N_MOE, N_EXPERTS, EXPERT_FFN_DIM, D_MODEL), DEEPNORM_BETA * EXPERT_FFN_DIM ** -0.5),
    }


def reference(x_prompt, x_sample, w_in, na_rpb, gqa_q_norm, gqa_k_norm, mla_q_norm, mla_w_uq, mla_kv_norm,
              mla_w_ukv, w_branch_a, w_branch_b, w_branch_c, w_out, ln1_g, ln1_b, ln2_g, ln2_b,
              ffn_w1, ffn_w3, ffn_w2, moe_router, moe_w1, moe_w3, moe_w2):
    y_prompt = _encode(x_prompt, w_in, na_rpb, gqa_q_norm, gqa_k_norm, mla_q_norm, mla_w_uq, mla_kv_norm, mla_w_ukv,
                       w_branch_a, w_branch_b, w_branch_c, w_out, ln1_g, ln1_b, ln2_g, ln2_b,
                       ffn_w1, ffn_w3, ffn_w2, moe_router, moe_w1, moe_w3, moe_w2)
    y_sample = _encode(x_sample, w_in, na_rpb, gqa_q_norm, gqa_k_norm, mla_q_norm, mla_w_uq, mla_kv_norm, mla_w_ukv,
                       w_branch_a, w_branch_b, w_branch_c, w_out, ln1_g, ln1_b, ln2_g, ln2_b,
                       ffn_w1, ffn_w3, ffn_w2, moe_router, moe_w1, moe_w3, moe_w2)
    return (y_prompt, y_sample)
```

```python
import functools

import numpy as np
import jax
import jax.numpy as jnp
from jax import lax
from jax.experimental import pallas as pl
from jax.experimental.pallas import tpu as pltpu

F32 = jnp.float32
BF16 = jnp.bfloat16

D_MODEL = 1024
GRID_W = 64
HEAD_DIM = 64
NA_WIN_H = 8
NA_WIN_W = 16
MLA_Q_LORA = 384
MLA_KV_LORA = 256
MLA_NOPE_DIM = 64
MLA_ROPE_DIM = 32
N_EXPERTS = 8
ROPE_THETA = 10000.0
RMS_EPS = 1e-6
LN_EPS = 1e-5
NEG_INF = -1e30

LANES = 128
ROWS_PER_STEP = 8

C_NAQ, C_NAK, C_NAV = 0, 512, 1024
C_GQ, C_GK, C_GV = 1536, 2048, 2176
C_CKV, C_KR, C_CQ, C_GATE = 2304, 2560, 2688, 3072
H_COLS = 6144

VMEM_LIMIT = 56 * 1024 * 1024


def _params(*sem):
    return pltpu.CompilerParams(dimension_semantics=sem, vmem_limit_bytes=VMEM_LIMIT)


def _pick(n, pref, mult=8):
    t = min(pref, n)
    while t > mult and (n % t or t % mult):
        t -= mult
    assert n % t == 0, (n, pref)
    return t


def _dot(a, b):
    return jnp.dot(a, b, preferred_element_type=F32)


def _dot_nt(a, b):
    return lax.dot_general(a, b, (((1,), (1,)), ((), ())), preferred_element_type=F32)


def _layernorm(z, g, b):
    mu = jnp.mean(z, axis=-1, keepdims=True)
    zc = z - mu
    var = jnp.mean(zc * zc, axis=-1, keepdims=True)
    return zc * lax.rsqrt(var + LN_EPS) * g + b


def _mm_kernel(x_ref, w_ref, o_ref):
    o_ref[...] = _dot(x_ref[...], w_ref[...]).astype(o_ref.dtype)


def _matmul(x, w, *, tm, tn, out_dtype=BF16):
    m, k = x.shape
    n = w.shape[1]
    return pl.pallas_call(
        _mm_kernel,
        grid=(m // tm, n // tn),
        in_specs=[pl.BlockSpec((tm, k), lambda i, j: (i, 0)),
                  pl.BlockSpec((k, tn), lambda i, j: (0, j))],
        out_specs=pl.BlockSpec((tm, tn), lambda i, j: (i, j)),
        out_shape=jax.ShapeDtypeStruct((m, n), out_dtype),
        compiler_params=_params("parallel", "parallel"),
        name="proj_in",
    )(x, w)


def _rope_apply(y, cos, sin_signed, lane):
    partner = jnp.where(lane % 32 < 16, pltpu.roll(y, LANES - 16, 1), pltpu.roll(y, 16, 1))
    return y * cos + partner * sin_signed


def _group_sumsq(x, ones_bd):
    s = x * x
    s_hi = s.astype(BF16)
    s_lo = (s - s_hi.astype(F32)).astype(BF16)
    return _dot(s_hi, ones_bd) + _dot(s_lo, ones_bd)


def _gqa_prep_kernel(q_ref, k_ref, v_ref, qg_ref, kg_ref, c_ref, s_ref, ones_ref, qo_ref, kd_ref, vd_ref):
    tm = q_ref.shape[0]
    lane = lax.broadcasted_iota(jnp.int32, (tm, LANES), 1)
    cos, sin_signed, ones_bd = c_ref[...], s_ref[...], ones_ref[...]

    def norm_rope(x, gain):
        y = x * lax.rsqrt(_group_sumsq(x, ones_bd) * (1.0 / HEAD_DIM) + RMS_EPS) * gain
        return _rope_apply(y, cos, sin_signed, lane)

    for j in range(q_ref.shape[1] // LANES):
        sl = slice(j * LANES, (j + 1) * LANES)
        qo_ref[:, sl] = norm_rope(q_ref[:, sl].astype(F32), qg_ref[...]).astype(BF16)
    low = lane < HEAD_DIM
    k = norm_rope(k_ref[...].astype(F32), kg_ref[...])
    k_sw = pltpu.roll(k, HEAD_DIM, 1)
    kd_ref[:, 0:LANES] = jnp.where(low, k, k_sw).astype(BF16)
    kd_ref[:, LANES:2 * LANES] = jnp.where(low, k_sw, k).astype(BF16)
    v = v_ref[...].astype(F32)
    v_sw = pltpu.roll(v, HEAD_DIM, 1)
    vd_ref[:, 0:LANES] = jnp.where(low, v, v_sw).astype(BF16)
    vd_ref[:, LANES:2 * LANES] = jnp.where(low, v_sw, v).astype(BF16)


def _gqa_prep(h, qg, kg, cos, sin_signed, ones_bd, *, tm):
    t = h.shape[0]
    row = lambda i: (i, 0)
    const = lambda i: (0, 0)
    return pl.pallas_call(
        _gqa_prep_kernel,
        grid=(t // tm,),
        in_specs=[pl.BlockSpec((tm, 512), lambda i: (i, C_GQ // 512)),
                  pl.BlockSpec((tm, LANES), lambda i: (i, C_GK // LANES)),
                  pl.BlockSpec((tm, LANES), lambda i: (i, C_GV // LANES)),
                  pl.BlockSpec((1, LANES), const), pl.BlockSpec((1, LANES), const),
                  pl.BlockSpec((tm, LANES), row), pl.BlockSpec((tm, LANES), row),
                  pl.BlockSpec((LANES, LANES), const)],
        out_specs=[pl.BlockSpec((tm, 512), row), pl.BlockSpec((tm, 256), row), pl.BlockSpec((tm, 256), row)],
        out_shape=[jax.ShapeDtypeStruct((t, 512), BF16), jax.ShapeDtypeStruct((t, 256), BF16),
                   jax.ShapeDtypeStruct((t, 256), BF16)],
        compiler_params=_params("parallel"),
        name="gqa_prep",
    )(h, h, h, qg, kg, cos, sin_signed, ones_bd)


def _mla_prep_kernel(cq_ref, ckv_ref, kr_ref, qn_ref, kvn_ref, wq_ref, wk_ref, wv_ref, c_ref, s_ref,
                     q_ref, k_ref, v_ref, *, scale):
    tm = cq_ref.shape[0]
    lane = lax.broadcasted_iota(jnp.int32, (tm, LANES), 1)
    cos, sin_signed = c_ref[...], s_ref[...]

    def rms(x, g):
        return x * lax.rsqrt(jnp.mean(x * x, axis=-1, keepdims=True) + RMS_EPS) * g

    cq = rms(cq_ref[...].astype(F32), qn_ref[...]).astype(BF16)
    ckv = rms(ckv_ref[...].astype(F32), kvn_ref[...]).astype(BF16)
    q = _dot(cq, wq_ref[...])
    k = _dot(ckv, wk_ref[...])
    v_ref[...] = _dot(ckv, wv_ref[...]).astype(BF16)
    k_rope = _rope_apply(kr_ref[...].astype(F32), cos, sin_signed, lane)
    for hd in range(q.shape[1] // LANES):
        sl = slice(hd * LANES, (hd + 1) * LANES)
        q_ref[:, sl] = (_rope_apply(q[:, sl], cos, sin_signed, lane) * scale).astype(BF16)
        k_ref[:, sl] = (k[:, sl] + k_rope).astype(BF16)


def _mla_prep(h, qn, kvn, wq, wk, wv, cos, sin_signed, *, tm):
    t = h.shape[0]
    row = lambda i: (i, 0)
    const = lambda i: (0, 0)
    scale = float((MLA_NOPE_DIM + MLA_ROPE_DIM) ** -0.5)
    return pl.pallas_call(
        functools.partial(_mla_prep_kernel, scale=scale),
        grid=(t // tm,),
        in_specs=[pl.BlockSpec((tm, MLA_Q_LORA), lambda i: (i, C_CQ // MLA_Q_LORA)),
                  pl.BlockSpec((tm, MLA_KV_LORA), lambda i: (i, C_CKV // MLA_KV_LORA)),
                  pl.BlockSpec((tm, LANES), lambda i: (i, C_KR // LANES)),
                  pl.BlockSpec((1, MLA_Q_LORA), const), pl.BlockSpec((1, MLA_KV_LORA), const),
                  pl.BlockSpec(wq.shape, const), pl.BlockSpec(wk.shape, const), pl.BlockSpec(wv.shape, const),
                  pl.BlockSpec((tm, LANES), row), pl.BlockSpec((tm, LANES), row)],
        out_specs=[pl.BlockSpec((tm, 1024), row), pl.BlockSpec((tm, 1024), row), pl.BlockSpec((tm, 512), row)],
        out_shape=[jax.ShapeDtypeStruct((t, 1024), BF16), jax.ShapeDtypeStruct((t, 1024), BF16),
                   jax.ShapeDtypeStruct((t, 512), BF16)],
        compiler_params=_params("parallel"),
        name="mla_prep",
    )(h, h, h, qn, kvn, wq, wk, wv, cos, sin_signed)


def _softmax_update(s, m, l):
    m_new = jnp.maximum(m, jnp.max(s, axis=-1, keepdims=True))
    a = jnp.exp(m - m_new)
    p = jnp.exp(s - m_new)
    return p, m_new, a * l + jnp.sum(p, axis=-1, keepdims=True), a


def _flash_pairs(q_pairs, kv_fn, n_chunks, tq):
    low = lax.broadcasted_iota(jnp.int32, (tq, LANES), 1) < HEAD_DIM
    neg = jnp.full((tq, 1), NEG_INF, F32)
    zero = jnp.zeros((tq, 1), F32)
    init = tuple((neg, zero, neg, zero, jnp.zeros((tq, LANES), F32)) for _ in q_pairs)

    def body(c, carry):
        out = []
        for (qe, qo), (ke, ko, ve, vo), (me, le, mo, lo, acc) in zip(q_pairs, kv_fn(c), carry):
            pe, me, le, ae = _softmax_update(_dot_nt(qe, ke), me, le)
            po, mo, lo, ao = _softmax_update(_dot_nt(qo, ko), mo, lo)
            acc = acc * jnp.where(low, ae, ao) + _dot(pe.astype(BF16), ve) + _dot(po.astype(BF16), vo)
            out.append((me, le, mo, lo, acc))
        return tuple(out)

    res = lax.fori_loop(0, n_chunks, body, init)
    return [acc / jnp.where(low, le, lo) for (_, le, _, lo, acc) in res]


def _split_lanes(x):
    low = lax.broadcasted_iota(jnp.int32, x.shape, 1) < HEAD_DIM
    zero = jnp.zeros_like(x)
    return jnp.where(low, x, zero), jnp.where(low, zero, x)


def _gqa_attn_kernel(q_ref, kd_ref, vd_ref, o_ref, *, tk):
    tq = q_ref.shape[0]
    n_pairs = q_ref.shape[1] // LANES
    q_pairs = [(q_ref[:, p * LANES:(p + 1) * LANES],) * 2 for p in range(n_pairs)]

    def kv_fn(c):
        rows = pl.ds(pl.multiple_of(c * tk, tk), tk)
        ke, ko = _split_lanes(kd_ref[rows, :])
        ve, vo = _split_lanes(vd_ref[rows, :])
        return [(ke, ko, ve, vo)] * n_pairs

    outs = _flash_pairs(q_pairs, kv_fn, kd_ref.shape[0] // tk, tq)
    for p, o in enumerate(outs):
        o_ref[:, p * LANES:(p + 1) * LANES] = o.astype(o_ref.dtype)


def _gqa_attn(q, kd, vd, *, tok_off, batch, seq, tq, tk):
    assert tok_off % seq == 0 and seq % tq == 0 and seq % tk == 0
    nq = seq // tq
    qmap = lambda b, g, i: (tok_off // tq + b * nq + i, g)
    kvmap = lambda b, g, i: (tok_off // seq + b, g)
    return pl.pallas_call(
        functools.partial(_gqa_attn_kernel, tk=tk),
        grid=(batch, 2, nq),
        in_specs=[pl.BlockSpec((tq, 256), qmap), pl.BlockSpec((seq, LANES), kvmap), pl.BlockSpec((seq, LANES), kvmap)],
        out_specs=pl.BlockSpec((tq, 256), lambda b, g, i: (b * nq + i, g)),
        out_shape=jax.ShapeDtypeStruct((batch * seq, 512), BF16),
        compiler_params=_params("parallel", "parallel", "parallel"),
        name="gqa_attn",
    )(q, kd, vd)


def _mla_attn_kernel(q_ref, k_ref, v_ref, o_ref, *, tk):
    tq = q_ref.shape[0]
    q_pairs = [(q_ref[:, 0:LANES], q_ref[:, LANES:2 * LANES])]

    def kv_fn(c):
        rows = pl.ds(pl.multiple_of(c * tk, tk), tk)
        ve, vo = _split_lanes(v_ref[rows, :])
        return [(k_ref[rows, 0:LANES], k_ref[rows, LANES:2 * LANES], ve, vo)]

    (o,) = _flash_pairs(q_pairs, kv_fn, k_ref.shape[0] // tk, tq)
    o_ref[...] = o.astype(o_ref.dtype)


def _mla_attn(q, k, v, *, tok_off, batch, seq, tq, tk):
    assert tok_off % seq == 0 and seq % tq == 0 and seq % tk == 0
    nq = seq // tq
    n_pairs = v.shape[1] // LANES
    kvmap = lambda b, p, i: (tok_off // seq + b, p)
    return pl.pallas_call(
        functools.partial(_mla_attn_kernel, tk=tk),
        grid=(batch, n_pairs, nq),
        in_specs=[pl.BlockSpec((tq, 256), lambda b, p, i: (tok_off // tq + b * nq + i, p)),
                  pl.BlockSpec((seq, 256), kvmap), pl.BlockSpec((seq, LANES), kvmap)],
        out_specs=pl.BlockSpec((tq, LANES), lambda b, p, i: (b * nq + i, p)),
        out_shape=jax.ShapeDtypeStruct((batch * seq, 512), BF16),
        compiler_params=_params("parallel", "parallel", "parallel"),
        name="mla_attn",
    )(q, k, v)


def _na_bias(rpb):
    c = np.arange(GRID_W)
    kc = np.arange(GRID_W)
    c0 = np.clip(c - NA_WIN_W // 2, 0, GRID_W - NA_WIN_W)
    ok = (kc[None, :] >= c0[:, None]) & (kc[None, :] < c0[:, None] + NA_WIN_W)
    dc = np.clip(kc[None, :] - c[:, None] + (NA_WIN_W - 1), 0, 2 * NA_WIN_W - 2)
    u = np.arange(NA_WIN_H)
    t = np.arange(NA_WIN_H)
    dr = t[None, :] - u[:, None] + (NA_WIN_H - 1)
    b = rpb[:, dr[:, :, None, None], dc[None, None, :, :]].astype(F32)
    b = jnp.where(ok[None, None, None], b, NEG_INF)
    return jnp.transpose(b, (0, 1, 3, 2, 4)).reshape(rpb.shape[0], NA_WIN_H, GRID_W, NA_WIN_H * GRID_W)


def _na_kernel(q_ref, kp_ref, kc_ref, kn_ref, vp_ref, vc_ref, vn_ref, b_ref, o_ref, kbuf, vbuf, *, rows):
    blk = ROWS_PER_STEP * GRID_W
    win = NA_WIN_H * GRID_W
    j = pl.program_id(2)
    for n, (kr, vr) in enumerate(((kp_ref, vp_ref), (kc_ref, vc_ref), (kn_ref, vn_ref))):
        kbuf[n * blk:(n + 1) * blk, :] = kr[...]
        vbuf[n * blk:(n + 1) * blk, :] = vr[...]
    low = lax.broadcasted_iota(jnp.int32, (GRID_W, LANES), 1) < HEAD_DIM

    def row_body(i, carry):
        r = j * ROWS_PER_STEP + i
        r0 = jnp.clip(r - NA_WIN_H // 2, 0, rows - NA_WIN_H)
        u = r - r0
        off = pl.multiple_of((r0 - j * ROWS_PER_STEP + ROWS_PER_STEP) * GRID_W, GRID_W)
        ke, ko = _split_lanes(kbuf[pl.ds(off, win), :])
        ve, vo = _split_lanes(vbuf[pl.ds(off, win), :])
        qrows = pl.ds(pl.multiple_of(i * GRID_W, GRID_W), GRID_W)
        q = q_ref[qrows, :]
        neg = jnp.full((GRID_W, 1), NEG_INF, F32)
        zero = jnp.zeros((GRID_W, 1), F32)
        pe, _, le, _ = _softmax_update(_dot_nt(q, ke) + b_ref[0, u], neg, zero)
        po, _, lo, _ = _softmax_update(_dot_nt(q, ko) + b_ref[1, u], neg, zero)
        acc = _dot(pe.astype(BF16), ve) + _dot(po.astype(BF16), vo)
        o_ref[qrows, :] = (acc / jnp.where(low, le, lo)).astype(o_ref.dtype)
        return carry

    lax.fori_loop(0, ROWS_PER_STEP, row_body, 0)


def _na_attn(h, bias, *, tok_off, batch, seq):
    rows = seq // GRID_W
    blk = ROWS_PER_STEP * GRID_W
    assert rows % ROWS_PER_STEP == 0 and rows >= NA_WIN_H and tok_off % blk == 0
    nb = rows // ROWS_PER_STEP
    n_pairs = bias.shape[0] // 2
    base = tok_off // blk

    def tokmap(col0, shift):
        return lambda b, p, j: (base + b * nb + jnp.clip(j + shift, 0, nb - 1), col0 // LANES + p)

    kv_specs = [pl.BlockSpec((blk, LANES), tokmap(c0, s)) for c0 in (C_NAK, C_NAV) for s in (-1, 0, 1)]
    return pl.pallas_call(
        functools.partial(_na_kernel, rows=rows),
        grid=(batch, n_pairs, nb),
        in_specs=[pl.BlockSpec((blk, LANES), tokmap(C_NAQ, 0))] + kv_specs
                 + [pl.BlockSpec((2, NA_WIN_H, GRID_W, NA_WIN_H * GRID_W), lambda b, p, j: (p, 0, 0, 0))],
        out_specs=pl.BlockSpec((blk, LANES), lambda b, p, j: (b * nb + j, p)),
        out_shape=jax.ShapeDtypeStruct((batch * seq, 512), BF16),
        scratch_shapes=[pltpu.VMEM((3 * blk, LANES), BF16), pltpu.VMEM((3 * blk, LANES), BF16)],
        compiler_params=_params("parallel", "parallel", "parallel"),
        name="na_attn",
    )(h, h, h, h, h, h, h, bias)


def _merge_kernel(ya_ref, yb_ref, yc_ref, g0_ref, g1_ref, g2_ref, x_ref, wa_ref, wb_ref, wc_ref, wo_ref,
                  lg_ref, lb_ref, o32_ref, o16_ref, *, alpha):
    def branch(y_ref, w_ref, g_ref):
        return jax.nn.sigmoid(g_ref[...].astype(F32)) * _dot(y_ref[...], w_ref[...])

    merged = branch(ya_ref, wa_ref, g0_ref) + branch(yb_ref, wb_ref, g1_ref) + branch(yc_ref, wc_ref, g2_ref)
    mix = _dot(merged.astype(BF16), wo_ref[...])
    y = _layernorm(alpha * x_ref[...] + mix, lg_ref[...], lb_ref[...])
    o32_ref[...] = y
    o16_ref[...] = y.astype(BF16)


def _merge(ya, yb, yc, h, x, wa, wb, wc, wo, lg, lb, *, alpha, tm):
    t = x.shape[0]
    row = lambda i: (i, 0)
    const = lambda i: (0, 0)
    gate = lambda n: pl.BlockSpec((tm, D_MODEL), lambda i: (i, C_GATE // D_MODEL + n))
    return pl.pallas_call(
        functools.partial(_merge_kernel, alpha=alpha),
        grid=(t // tm,),
        in_specs=[pl.BlockSpec((tm, 512), row)] * 3 + [gate(0), gate(1), gate(2), pl.BlockSpec((tm, D_MODEL), row)]
                 + [pl.BlockSpec((512, D_MODEL), const)] * 3 + [pl.BlockSpec((D_MODEL, D_MODEL), const)]
                 + [pl.BlockSpec((1, D_MODEL), const)] * 2,
        out_specs=[pl.BlockSpec((tm, D_MODEL), row)] * 2,
        out_shape=[jax.ShapeDtypeStruct((t, D_MODEL), F32), jax.ShapeDtypeStruct((t, D_MODEL), BF16)],
        compiler_params=_params("parallel"),
        name="merge_ln1",
    )(ya, yb, yc, h, h, h, x, wa, wb, wc, wo, lg, lb)


def _ffn_body(tv_ref, x_ref, w1_ref, w3_ref, w2_ref, xb, acc):
    i, j = pl.program_id(0), pl.program_id(1)

    @pl.when(j == 0)
    def _():
        acc[...] = jnp.zeros_like(acc)
        xb[...] = x_ref[...].astype(BF16)

    @pl.when(tv_ref[i] != 0)
    def _():
        x = xb[...]
        a = _dot(x, w1_ref[0])
        b = _dot(x, w3_ref[0])
        mid = (a * jax.nn.sigmoid(a)) * b
        acc[...] += _dot(mid.astype(BF16), w2_ref[0])


def _ffn_dense_kernel(te_ref, tv_ref, x_ref, w1_ref, w3_ref, w2_ref, r_ref, lg_ref, lb_ref, o32_ref, o16_ref,
                      xb, acc, *, alpha):
    _ffn_body(tv_ref, x_ref, w1_ref, w3_ref, w2_ref, xb, acc)

    @pl.when(pl.program_id(1) == pl.num_programs(1) - 1)
    def _():
        y = _layernorm(alpha * r_ref[...] + acc[...], lg_ref[...], lb_ref[...])
        o32_ref[...] = y
        o16_ref[...] = y.astype(BF16)


def _ffn_group_kernel(te_ref, tv_ref, x_ref, w1_ref, w3_ref, w2_ref, gw_ref, o_ref, xb, acc):
    _ffn_body(tv_ref, x_ref, w1_ref, w3_ref, w2_ref, xb, acc)

    @pl.when(pl.program_id(1) == pl.num_programs(1) - 1)
    def _():
        o_ref[...] = acc[...] * gw_ref[...]


def _ffn_specs(tm, tf, d, nf):
    fidx = lambda j, v: j * v + (nf - 1) * (1 - v)
    return [pl.BlockSpec((tm, d), lambda i, j, te, tv: (i, 0)),
            pl.BlockSpec((1, d, tf), lambda i, j, te, tv: (te[i], 0, fidx(j, tv[i]))),
            pl.BlockSpec((1, d, tf), lambda i, j, te, tv: (te[i], 0, fidx(j, tv[i]))),
            pl.BlockSpec((1, tf, d), lambda i, j, te, tv: (te[i], fidx(j, tv[i]), 0))]


def _ffn_dense(x16, x32, w1, w3, w2, lg, lb, *, alpha, tm, tf):
    t, d = x16.shape
    nt, nf = t // tm, w1.shape[2] // tf
    row = lambda i, j, te, tv: (i, 0)
    const = lambda i, j, te, tv: (0, 0)
    te = jnp.zeros((nt,), jnp.int32)
    tv = jnp.ones((nt,), jnp.int32)
    return pl.pallas_call(
        functools.partial(_ffn_dense_kernel, alpha=alpha),
        grid_spec=pltpu.PrefetchScalarGridSpec(
            num_scalar_prefetch=2, grid=(nt, nf),
            in_specs=_ffn_specs(tm, tf, d, nf) + [pl.BlockSpec((tm, d), row), pl.BlockSpec((1, d), const),
                                                   pl.BlockSpec((1, d), const)],
            out_specs=[pl.BlockSpec((tm, d), row)] * 2,
            scratch_shapes=[pltpu.VMEM((tm, d), BF16), pltpu.VMEM((tm, d), F32)]),
        out_shape=[jax.ShapeDtypeStruct((t, d), F32), jax.ShapeDtypeStruct((t, d), BF16)],
        compiler_params=_params("parallel", "arbitrary"),
        name="ffn_dense",
    )(te, tv, x16, w1, w3, w2, x32, lg, lb)


def _ffn_grouped(xs, gw, te, tv, w1, w3, w2, *, tm, tf):
    n, d = xs.shape
    nt, nf = n // tm, w1.shape[2] // tf
    row = lambda i, j, te, tv: (i, 0)
    return pl.pallas_call(
        _ffn_group_kernel,
        grid_spec=pltpu.PrefetchScalarGridSpec(
            num_scalar_prefetch=2, grid=(nt, nf),
            in_specs=_ffn_specs(tm, tf, d, nf) + [pl.BlockSpec((tm, 1), row)],
            out_specs=pl.BlockSpec((tm, d), row),
            scratch_shapes=[pltpu.VMEM((tm, d), BF16), pltpu.VMEM((tm, d), F32)]),
        out_shape=jax.ShapeDtypeStruct((n, d), F32),
        compiler_params=_params("parallel", "arbitrary"),
        name="ffn_grouped",
    )(te, tv, xs, w1, w3, w2, gw)


def _router_kernel(x_ref, wh_ref, wl_ref, idx_ref, wt_ref):
    x = x_ref[...]
    xh = x.astype(BF16)
    xl = (x - xh.astype(F32)).astype(BF16)
    logits = _dot(xh, wh_ref[...]) + _dot(xl, wh_ref[...]) + _dot(xh, wl_ref[...])
    lane = lax.broadcasted_iota(jnp.int32, logits.shape, 1)
    ninf = jnp.float32(-jnp.inf)
    l1 = jnp.where(lane < N_EXPERTS, logits, ninf)
    m1 = jnp.max(l1, axis=-1, keepdims=True)
    i1 = jnp.min(jnp.where(l1 == m1, lane, LANES), axis=-1, keepdims=True)
    l2 = jnp.where(lane == i1, ninf, l1)
    m2 = jnp.max(l2, axis=-1, keepdims=True)
    i2 = jnp.min(jnp.where(l2 == m2, lane, LANES), axis=-1, keepdims=True)
    e = jnp.exp(m2 - m1)
    den = 1.0 + e
    idx_ref[...] = jnp.where(lane == 0, i1, jnp.where(lane == 1, i2, 0))
    wt_ref[...] = jnp.where(lane == 0, 1.0 / den, jnp.where(lane == 1, e / den, 0.0))


def _router(x32, wh, wl, *, tm):
    t, d = x32.shape
    row = lambda i: (i, 0)
    const = lambda i: (0, 0)
    return pl.pallas_call(
        _router_kernel,
        grid=(t // tm,),
        in_specs=[pl.BlockSpec((tm, d), row), pl.BlockSpec((d, LANES), const), pl.BlockSpec((d, LANES), const)],
        out_specs=[pl.BlockSpec((tm, LANES), row)] * 2,
        out_shape=[jax.ShapeDtypeStruct((t, LANES), jnp.int32), jax.ShapeDtypeStruct((t, LANES), F32)],
        compiler_params=_params("parallel"),
        name="router",
    )(x32, wh, wl)


def _row_copy(src_hbm, src_row, dst_ref, dst_row, sem):
    return pltpu.make_async_copy(src_hbm.at[pl.ds(src_row, 1)], dst_ref.at[pl.ds(dst_row, 1)], sem)


def _gather_kernel(src_ref, x_hbm, o_ref, sem):
    tm = o_ref.shape[0]
    base = pl.program_id(0) * tm

    def issue(r, c):
        _row_copy(x_hbm, src_ref[base + r], o_ref, r, sem).start()
        return c

    def wait(r, c):
        _row_copy(x_hbm, 0, o_ref, r, sem).wait()
        return c

    lax.fori_loop(0, tm, issue, 0)
    lax.fori_loop(0, tm, wait, 0)


def _gather_rows(x32, src, *, tm):
    n = src.shape[0]
    d = x32.shape[1]
    return pl.pallas_call(
        _gather_kernel,
        grid_spec=pltpu.PrefetchScalarGridSpec(
            num_scalar_prefetch=1, grid=(n // tm,),
            in_specs=[pl.BlockSpec(memory_space=pl.ANY)],
            out_specs=pl.BlockSpec((tm, d), lambda i, src: (i, 0)),
            scratch_shapes=[pltpu.SemaphoreType.DMA(())]),
        out_shape=jax.ShapeDtypeStruct((n, d), x32.dtype),
        compiler_params=_params("arbitrary"),
        name="moe_gather",
    )(src, x32)


def _combine_kernel(pos_ref, x_ref, ys_hbm, lg_ref, lb_ref, o32_ref, o16_ref, buf0, buf1, sem, *, alpha):
    tm = x_ref.shape[0]
    base = pl.program_id(0) * tm

    def issue(r, c):
        _row_copy(ys_hbm, pos_ref[2 * (base + r)], buf0, r, sem).start()
        _row_copy(ys_hbm, pos_ref[2 * (base + r) + 1], buf1, r, sem).start()
        return c

    def wait(r, c):
        _row_copy(ys_hbm, 0, buf0, r, sem).wait()
        _row_copy(ys_hbm, 0, buf1, r, sem).wait()
        return c

    lax.fori_loop(0, tm, issue, 0)
    lax.fori_loop(0, tm, wait, 0)
    y = _layernorm(alpha * x_ref[...] + (buf0[...] + buf1[...]), lg_ref[...], lb_ref[...])
    o32_ref[...] = y
    o16_ref[...] = y.astype(BF16)


def _combine(pos, x32, ys, lg, lb, *, alpha, tm):
    t, d = x32.shape
    row = lambda i, pos: (i, 0)
    const = lambda i, pos: (0, 0)
    return pl.pallas_call(
        functools.partial(_combine_kernel, alpha=alpha),
        grid_spec=pltpu.PrefetchScalarGridSpec(
            num_scalar_prefetch=1, grid=(t // tm,),
            in_specs=[pl.BlockSpec((tm, d), row), pl.BlockSpec(memory_space=pl.ANY),
                      pl.BlockSpec((1, d), const), pl.BlockSpec((1, d), const)],
            out_specs=[pl.BlockSpec((tm, d), row)] * 2,
            scratch_shapes=[pltpu.VMEM((tm, d), F32), pltpu.VMEM((tm, d), F32), pltpu.SemaphoreType.DMA(())]),
        out_shape=[jax.ShapeDtypeStruct((t, d), F32), jax.ShapeDtypeStruct((t, d), BF16)],
        compiler_params=_params("arbitrary"),
        name="moe_combine",
    )(pos, x32, ys, lg, lb)


def _route_meta(idx, wts, tm):
    t = idx.shape[0]
    a = 2 * t
    e = idx.reshape(a)
    w = wts.reshape(a)
    onehot = (e[:, None] == jnp.arange(N_EXPERTS, dtype=jnp.int32)[None, :]).astype(jnp.int32)
    csum = jnp.cumsum(onehot, axis=0)
    rank = jnp.sum((csum - onehot) * onehot, axis=1)
    cnt = csum[-1]
    pcnt = ((cnt + tm - 1) // tm) * tm
    pend = jnp.cumsum(pcnt)
    pos = (pend - pcnt)[e] + rank
    n_rows = a + N_EXPERTS * tm
    src = jnp.zeros((n_rows,), jnp.int32).at[pos].set(jnp.arange(a, dtype=jnp.int32) // 2)
    gw = jnp.zeros((n_rows,), F32).at[pos].set(w)
    start = jnp.arange(n_rows // tm, dtype=jnp.int32) * tm
    te = jnp.minimum(jnp.searchsorted(pend, start, side="right"), N_EXPERTS - 1).astype(jnp.int32)
    tv = (start < pend[-1]).astype(jnp.int32)
    return pos.astype(jnp.int32), src, gw.reshape(n_rows, 1), te, tv


def _prep_w_in(w):
    d = w.shape[0]
    z = lambda n: jnp.zeros((d, n), w.dtype)
    na_q, na_k, na_v = w[:, 0:512] * (HEAD_DIM ** -0.5), w[:, 512:1024], w[:, 1024:1536]
    g_q, g_k, g_v = w[:, 1536:2048], w[:, 2048:2176], w[:, 2176:2304]
    c_q, c_kv, k_r, gate = w[:, 2304:2688], w[:, 2688:2944], w[:, 2944:2976], w[:, 2976:]
    kr_blk = jnp.concatenate([z(MLA_NOPE_DIM), k_r, z(LANES - MLA_NOPE_DIM - MLA_ROPE_DIM)], axis=1)
    out = jnp.concatenate([na_q, na_k, na_v, g_q, g_k, g_v, c_kv, kr_blk, c_q, gate], axis=1).astype(BF16)
    assert out.shape[1] == H_COLS
    return out


def _prep_mla_w(w_uq, w_ukv):
    heads = w_uq.shape[1] // (MLA_NOPE_DIM + MLA_ROPE_DIM)
    wq = w_uq.reshape(MLA_Q_LORA, heads, MLA_NOPE_DIM + MLA_ROPE_DIM)
    wq = jnp.pad(wq, ((0, 0), (0, 0), (0, LANES - MLA_NOPE_DIM - MLA_ROPE_DIM))).reshape(MLA_Q_LORA, heads * LANES)
    wkv = w_ukv.reshape(MLA_KV_LORA, heads, LANES)
    wk = jnp.pad(wkv[:, :, :MLA_NOPE_DIM], ((0, 0), (0, 0), (0, LANES - MLA_NOPE_DIM))).reshape(MLA_KV_LORA, heads * LANES)
    wv = wkv[:, :, MLA_NOPE_DIM:].reshape(MLA_KV_LORA, heads * HEAD_DIM)
    return wq.astype(BF16), wk.astype(BF16), wv.astype(BF16)


def _rope_tables(seq):
    pos = jnp.arange(seq, dtype=jnp.int32)
    half = 16
    inv_freq = ROPE_THETA ** (-jnp.arange(half, dtype=F32) / half)

    def cs(p):
        ang = p.astype(F32)[:, None] * inv_freq[None, :]
        return jnp.cos(ang), jnp.sin(ang)

    cr, sr = cs(pos // GRID_W)
    cc, sc = cs(pos % GRID_W)
    cp, sp = cs(pos)
    one = lambda n: jnp.ones((seq, n), F32)
    zero = lambda n: jnp.zeros((seq, n), F32)
    g_cos = jnp.tile(jnp.concatenate([cr, cr, cc, cc], axis=1), (1, 2))
    g_sin = jnp.tile(jnp.concatenate([-sr, sr, -sc, sc], axis=1), (1, 2))
    m_cos = jnp.concatenate([one(MLA_NOPE_DIM), cp, cp, one(32)], axis=1)
    m_sin = jnp.concatenate([zero(MLA_NOPE_DIM), -sp, sp, zero(32)], axis=1)
    return g_cos, g_sin, m_cos, m_sin


def kernel(x_prompt, x_sample, w_in, na_rpb, gqa_q_norm, gqa_k_norm, mla_q_norm, mla_w_uq, mla_kv_norm, mla_w_ukv, w_branch_a, w_branch_b, w_branch_c, w_out, ln1_g, ln1_b, ln2_g, ln2_b, ffn_w1, ffn_w3, ffn_w2, moe_router, moe_w1, moe_w3, moe_w2):
    depth = w_in.shape[0]
    alpha = float((2 * depth) ** 0.25)
    groups = [(x_prompt.shape[0], x_prompt.shape[1]), (x_sample.shape[0], x_sample.shape[1])]
    d = x_prompt.shape[2]
    assert d == D_MODEL
    x32 = jnp.concatenate([x_prompt.reshape(-1, d), x_sample.reshape(-1, d)], axis=0)
    t = x32.shape[0]
    x16 = x32.astype(BF16)

    tabs = [_rope_tables(s) for _, s in groups]
    g_cos, g_sin, m_cos, m_sin = [jnp.concatenate([jnp.tile(tabs[g][n], (groups[g][0], 1)) for g in range(2)], axis=0)
                                  for n in range(4)]
    blk = np.kron(np.eye(2), np.ones((HEAD_DIM, HEAD_DIM)))
    ones_bd = jnp.asarray(blk, BF16)

    tm_proj = _pick(t, 1024)
    tm_tok = _pick(t, 512)
    tm_moe = _pick(t, 512)
    row2 = lambda v: v.reshape(1, -1).astype(F32)

    for i in range(depth):
        h = _matmul(x16, _prep_w_in(w_in[i]), tm=tm_proj, tn=512)
        qg = row2(jnp.tile(gqa_q_norm[i] * (HEAD_DIM ** -0.5), 2))
        kg = row2(jnp.tile(gqa_k_norm[i], 2))
        gq, gkd, gvd = _gqa_prep(h, qg, kg, g_cos, g_sin, ones_bd, tm=tm_tok)
        wq, wk, wv = _prep_mla_w(mla_w_uq[i], mla_w_ukv[i])
        mq, mk, mv = _mla_prep(h, row2(mla_q_norm[i]), row2(mla_kv_norm[i]), wq, wk, wv, m_cos, m_sin, tm=tm_tok)
        bias = _na_bias(na_rpb[i])
        ya, yb, yc = [], [], []
        off = 0
        for batch, seq in groups:
            tq = _pick(seq, 256)
            tk = _pick(seq, 512)
            ya.append(_na_attn(h, bias, tok_off=off, batch=batch, seq=seq))
            yb.append(_gqa_attn(gq, gkd, gvd, tok_off=off, batch=batch, seq=seq, tq=tq, tk=tk))
            yc.append(_mla_attn(mq, mk, mv, tok_off=off, batch=batch, seq=seq, tq=tq, tk=tk))
            off += batch * seq
        ya, yb, yc = [jnp.concatenate(v, axis=0) for v in (ya, yb, yc)]
        x32, x16 = _merge(ya, yb, yc, h, x32, w_branch_a[i].astype(BF16), w_branch_b[i].astype(BF16),
                          w_branch_c[i].astype(BF16), w_out[i].astype(BF16), row2(ln1_g[i]), row2(ln1_b[i]),
                          alpha=alpha, tm=tm_tok)
        j = i // 2
        if i % 2 == 0:
            f = ffn_w1.shape[2]
            x32, x16 = _ffn_dense(x16, x32, ffn_w1[j:j + 1].astype(BF16), ffn_w3[j:j + 1].astype(BF16),
                                  ffn_w2[j:j + 1].astype(BF16), row2(ln2_g[i]), row2(ln2_b[i]),
                                  alpha=alpha, tm=tm_tok, tf=_pick(f, 1408, LANES))
        else:
            rw = jnp.pad(moe_router[j], ((0, 0), (0, LANES - N_EXPERTS)))
            rw_hi = rw.astype(BF16)
            rw_lo = (rw - rw_hi.astype(F32)).astype(BF16)
            idx, wts = _router(x32, rw_hi, rw_lo, tm=tm_tok)
            pos, src, gw, te, tv = _route_meta(idx[:, :2], wts[:, :2], tm_moe)
            xs = _gather_rows(x32, src, tm=tm_moe)
            f = moe_w1.shape[3]
            ys = _ffn_grouped(xs, gw, te, tv, moe_w1[j].astype(BF16), moe_w3[j].astype(BF16),
                              moe_w2[j].astype(BF16), tm=tm_moe, tf=_pick(f, 896, LANES))
            x32, x16 = _combine(pos, x32, ys, row2(ln2_g[i]), row2(ln2_b[i]), alpha=alpha, tm=_pick(t, 256))

    tp = groups[0][0] * groups[0][1]
    return (x32[:tp].reshape(x_prompt.shape), x32[tp:].reshape(x_sample.shape))
```

```python
import functools
import math

import numpy as np
import jax
import jax.numpy as jnp
from jax import lax
from jax.experimental import pallas as pl
from jax.experimental.pallas import tpu as pltpu

F32 = jnp.float32
BF16 = jnp.bfloat16

D_MODEL = 1024
GRID_W = 64
HEAD_DIM = 64
NA_WIN_H = 8
NA_WIN_W = 16
MLA_Q_LORA = 384
MLA_KV_LORA = 256
MLA_NOPE_DIM = 64
MLA_ROPE_DIM = 32
N_EXPERTS = 8
ROPE_THETA = 10000.0
RMS_EPS = 1e-6
LN_EPS = 1e-5
NEG_INF = -1e30
LOG2E = math.log2(math.e)

LANES = 128
BF16_ROWS = 16
ROWS_PER_STEP = 8
DMA_UNROLL = 8
SUB_KEYS = 256
QK_AHEAD = 3
TILES_PER_BODY = 16

C_NAQ, C_NAK, C_NAV = 0, 512, 1024
C_GQ, C_GK, C_GV = 1536, 2048, 2176
C_CKV, C_KR, C_CQ, C_GATE = 2304, 2560, 2688, 3072
H_COLS = 6144

VMEM_LIMIT = 56 * 1024 * 1024


def _params(*sem):
    return pltpu.CompilerParams(dimension_semantics=sem, vmem_limit_bytes=VMEM_LIMIT)


def _pick(n, pref, mult=8):
    t = min(pref, n)
    while t > mult and (n % t or t % mult):
        t -= mult
    assert n % t == 0, (n, pref)
    return t


def _dot(a, b):
    return jnp.dot(a, b, preferred_element_type=F32)


def _dot_nt(a, b):
    return lax.dot_general(a, b, (((1,), (1,)), ((), ())), preferred_element_type=F32)


def _dot_tn(a, b):
    return lax.dot_general(a, b, (((0,), (0,)), ((), ())), preferred_element_type=F32)


def _layernorm(z, g, b):
    mu = jnp.mean(z, axis=-1, keepdims=True)
    zc = z - mu
    var = jnp.mean(zc * zc, axis=-1, keepdims=True)
    return zc * lax.rsqrt(var + LN_EPS) * g + b


def _mm_kernel(x_ref, w_ref, o_ref):
    o_ref[...] = _dot(x_ref[...], w_ref[...]).astype(o_ref.dtype)


def _matmul(x, w, *, tm, tn, out_dtype=BF16):
    m, k = x.shape
    n = w.shape[1]
    return pl.pallas_call(
        _mm_kernel,
        grid=(m // tm, n // tn),
        in_specs=[pl.BlockSpec((tm, k), lambda i, j: (i, 0)),
                  pl.BlockSpec((k, tn), lambda i, j: (0, j))],
        out_specs=pl.BlockSpec((tm, tn), lambda i, j: (i, j)),
        out_shape=jax.ShapeDtypeStruct((m, n), out_dtype),
        compiler_params=_params("parallel", "parallel"),
        name="proj_in",
    )(x, w)


def _rope_apply(y, cos, sin_signed, lane):
    partner = jnp.where(lane % 32 < 16, pltpu.roll(y, LANES - 16, 1), pltpu.roll(y, 16, 1))
    return y * cos + partner * sin_signed


def _group_sumsq(x, ones_bd):
    s = x * x
    s_hi = s.astype(BF16)
    s_lo = (s - s_hi.astype(F32)).astype(BF16)
    return _dot(s_hi, ones_bd) + _dot(s_lo, ones_bd)


def _gqa_prep_kernel(q_ref, k_ref, v_ref, qg_ref, kg_ref, c_ref, s_ref, ones_ref, qo_ref, kd_ref, vt_ref):
    tm = q_ref.shape[0]
    lane = lax.broadcasted_iota(jnp.int32, (tm, LANES), 1)
    cos, sin_signed, ones_bd = c_ref[...], s_ref[...], ones_ref[...]

    def norm_rope(x, gain):
        y = x * lax.rsqrt(_group_sumsq(x, ones_bd) * (1.0 / HEAD_DIM) + RMS_EPS) * gain
        return _rope_apply(y, cos, sin_signed, lane)

    for j in range(q_ref.shape[1] // LANES):
        sl = slice(j * LANES, (j + 1) * LANES)
        qo_ref[:, sl] = norm_rope(q_ref[:, sl].astype(F32), qg_ref[...]).astype(BF16)
    low = lane < HEAD_DIM
    k = norm_rope(k_ref[...].astype(F32), kg_ref[...])
    k_sw = pltpu.roll(k, HEAD_DIM, 1)
    kd_ref[:, 0:LANES] = jnp.where(low, k, k_sw).astype(BF16)
    kd_ref[:, LANES:2 * LANES] = jnp.where(low, k_sw, k).astype(BF16)
    vt_ref[...] = v_ref[...].astype(F32).T.astype(BF16)


def _gqa_prep(h, qg, kg, cos, sin_signed, ones_bd, *, tm):
    t = h.shape[0]
    row = lambda i: (i, 0)
    const = lambda i: (0, 0)
    return pl.pallas_call(
        _gqa_prep_kernel,
        grid=(t // tm,),
        in_specs=[pl.BlockSpec((tm, 512), lambda i: (i, C_GQ // 512)),
                  pl.BlockSpec((tm, LANES), lambda i: (i, C_GK // LANES)),
                  pl.BlockSpec((tm, LANES), lambda i: (i, C_GV // LANES)),
                  pl.BlockSpec((1, LANES), const), pl.BlockSpec((1, LANES), const),
                  pl.BlockSpec((tm, LANES), row), pl.BlockSpec((tm, LANES), row),
                  pl.BlockSpec((LANES, LANES), const)],
        out_specs=[pl.BlockSpec((tm, 512), row), pl.BlockSpec((tm, 256), row),
                   pl.BlockSpec((LANES, tm), lambda i: (0, i))],
        out_shape=[jax.ShapeDtypeStruct((t, 512), BF16), jax.ShapeDtypeStruct((t, 256), BF16),
                   jax.ShapeDtypeStruct((LANES, t), BF16)],
        compiler_params=_params("parallel"),
        name="gqa_prep",
    )(h, h, h, qg, kg, cos, sin_signed, ones_bd)


def _mla_prep_kernel(cq_ref, ckv_ref, kr_ref, qn_ref, kvn_ref, wq_ref, wk_ref, wvt_ref, c_ref, s_ref,
                     q_ref, k_ref, vt_ref, *, scale):
    tm = cq_ref.shape[0]
    lane = lax.broadcasted_iota(jnp.int32, (tm, LANES), 1)
    cos, sin_signed = c_ref[...], s_ref[...]

    def rms(x, g):
        return x * lax.rsqrt(jnp.mean(x * x, axis=-1, keepdims=True) + RMS_EPS) * g

    cq = rms(cq_ref[...].astype(F32), qn_ref[...]).astype(BF16)
    ckv = rms(ckv_ref[...].astype(F32), kvn_ref[...]).astype(BF16)
    q = _dot(cq, wq_ref[...])
    k = _dot(ckv, wk_ref[...])
    vt_ref[...] = _dot_nt(wvt_ref[...], ckv).astype(BF16)
    k_rope = _rope_apply(kr_ref[...].astype(F32), cos, sin_signed, lane)
    for hd in range(q.shape[1] // LANES):
        sl = slice(hd * LANES, (hd + 1) * LANES)
        q_ref[:, sl] = (_rope_apply(q[:, sl], cos, sin_signed, lane) * scale).astype(BF16)
        k_ref[:, sl] = (k[:, sl] + k_rope).astype(BF16)


def _mla_prep(h, qn, kvn, wq, wk, wvt, cos, sin_signed, *, tm):
    t = h.shape[0]
    row = lambda i: (i, 0)
    const = lambda i: (0, 0)
    scale = float((MLA_NOPE_DIM + MLA_ROPE_DIM) ** -0.5 * LOG2E)
    return pl.pallas_call(
        functools.partial(_mla_prep_kernel, scale=scale),
        grid=(t // tm,),
        in_specs=[pl.BlockSpec((tm, MLA_Q_LORA), lambda i: (i, C_CQ // MLA_Q_LORA)),
                  pl.BlockSpec((tm, MLA_KV_LORA), lambda i: (i, C_CKV // MLA_KV_LORA)),
                  pl.BlockSpec((tm, LANES), lambda i: (i, C_KR // LANES)),
                  pl.BlockSpec((1, MLA_Q_LORA), const), pl.BlockSpec((1, MLA_KV_LORA), const),
                  pl.BlockSpec(wq.shape, const), pl.BlockSpec(wk.shape, const), pl.BlockSpec(wvt.shape, const),
                  pl.BlockSpec((tm, LANES), row), pl.BlockSpec((tm, LANES), row)],
        out_specs=[pl.BlockSpec((tm, 1024), row), pl.BlockSpec((tm, 1024), row),
                   pl.BlockSpec((512, tm), lambda i: (0, i))],
        out_shape=[jax.ShapeDtypeStruct((t, 1024), BF16), jax.ShapeDtypeStruct((t, 1024), BF16),
                   jax.ShapeDtypeStruct((512, t), BF16)],
        compiler_params=_params("parallel"),
        name="mla_prep",
    )(h, h, h, qn, kvn, wq, wk, wvt, cos, sin_signed)


def _flash_t(q_ops, kv_fn, n_chunks, n_sub, tq):
    init = tuple((jnp.full((1, tq), NEG_INF, F32), jnp.zeros((HEAD_DIM + BF16_ROWS, tq), F32)) for _ in q_ops)

    def body(c, carry):
        carry = list(carry)
        tiles = [(s, hd) for s in range(n_sub) for hd in range(len(q_ops))]
        kvs = {}

        def scores(s, hd):
            if s not in kvs:
                kvs[s] = kv_fn(c, s)
            return _dot_nt(kvs[s][hd][0], q_ops[hd])

        pending = [scores(*tl) for tl in tiles[:QK_AHEAD]]
        for n, (s, hd) in enumerate(tiles):
            if n + QK_AHEAD < len(tiles):
                pending.append(scores(*tiles[n + QK_AHEAD]))
            st = pending.pop(0)
            m, acc = carry[hd]
            m_new = jnp.maximum(m, jnp.max(st, axis=0, keepdims=True))
            alpha = jnp.exp2(m - m_new)
            pt = jnp.exp2(st - m_new).astype(BF16)
            carry[hd] = (m_new, acc * alpha + _dot(kvs[s][hd][1], pt))
        return tuple(carry)

    res = lax.fori_loop(0, n_chunks, body, init)
    return [acc[0:HEAD_DIM] / acc[HEAD_DIM:HEAD_DIM + 1] for _, acc in res]


def _gqa_attn_kernel(q_ref, kd_ref, vt_ref, o_ref, *, tk):
    tq = q_ref.shape[0]
    n_heads = q_ref.shape[1] // HEAD_DIM
    low = lax.broadcasted_iota(jnp.int32, (tq, LANES), 1) < HEAD_DIM
    q_ops = []
    for hd in range(n_heads):
        blk = q_ref[:, (hd // 2) * LANES:(hd // 2 + 1) * LANES]
        zero = jnp.zeros_like(blk)
        q_ops.append(jnp.where(low, blk, zero) if hd % 2 == 0 else jnp.where(low, zero, blk))
    ones = jnp.ones((BF16_ROWS, SUB_KEYS), BF16)

    def kv_fn(c, s):
        start = pl.multiple_of(c * tk + s * SUB_KEYS, SUB_KEYS)
        k = kd_ref[pl.ds(start, SUB_KEYS), :]
        v_aug = jnp.concatenate([vt_ref[:, pl.ds(start, SUB_KEYS)], ones], axis=0)
        return [(k, v_aug)] * n_heads

    outs = _flash_t(q_ops, kv_fn, kd_ref.shape[0] // tk, tk // SUB_KEYS, tq)
    for hd, o in enumerate(outs):
        o_ref[hd * HEAD_DIM:(hd + 1) * HEAD_DIM, :] = o.astype(o_ref.dtype)


def _gqa_attn(q, kd, vt, *, tok_off, batch, seq, tq, tk):
    assert tok_off % seq == 0 and seq % tq == 0 and seq % tk == 0
    nq = seq // tq
    return pl.pallas_call(
        functools.partial(_gqa_attn_kernel, tk=tk),
        grid=(batch, 2, nq),
        in_specs=[pl.BlockSpec((tq, 256), lambda b, g, i: (tok_off // tq + b * nq + i, g)),
                  pl.BlockSpec((seq, LANES), lambda b, g, i: (tok_off // seq + b, g)),
                  pl.BlockSpec((HEAD_DIM, seq), lambda b, g, i: (g, tok_off // seq + b))],
        out_specs=pl.BlockSpec((256, tq), lambda b, g, i: (g, b * nq + i)),
        out_shape=jax.ShapeDtypeStruct((512, batch * seq), BF16),
        compiler_params=_params("parallel", "parallel", "parallel"),
        name="gqa_attn",
    )(q, kd, vt)


def _mla_attn_kernel(q_ref, k_ref, vt_ref, o_ref, *, tk):
    tq = q_ref.shape[0]
    n_heads = q_ref.shape[1] // LANES
    q_ops = [q_ref[:, hd * LANES:(hd + 1) * LANES] for hd in range(n_heads)]
    ones = jnp.ones((BF16_ROWS, SUB_KEYS), BF16)

    def kv_fn(c, s):
        rows = pl.ds(pl.multiple_of(c * tk + s * SUB_KEYS, SUB_KEYS), SUB_KEYS)
        return [(k_ref[rows, hd * LANES:(hd + 1) * LANES],
                 jnp.concatenate([vt_ref[hd * HEAD_DIM:(hd + 1) * HEAD_DIM, rows], ones], axis=0))
                for hd in range(n_heads)]

    outs = _flash_t(q_ops, kv_fn, k_ref.shape[0] // tk, tk // SUB_KEYS, tq)
    for hd, o in enumerate(outs):
        o_ref[hd * HEAD_DIM:(hd + 1) * HEAD_DIM, :] = o.astype(o_ref.dtype)


def _mla_attn(q, k, vt, *, tok_off, batch, seq, tq, tk):
    assert tok_off % seq == 0 and seq % tq == 0 and seq % tk == 0
    nq = seq // tq
    n_pairs = vt.shape[0] // LANES
    return pl.pallas_call(
        functools.partial(_mla_attn_kernel, tk=tk),
        grid=(batch, n_pairs, nq),
        in_specs=[pl.BlockSpec((tq, 256), lambda b, p, i: (tok_off // tq + b * nq + i, p)),
                  pl.BlockSpec((seq, 256), lambda b, p, i: (tok_off // seq + b, p)),
                  pl.BlockSpec((LANES, seq), lambda b, p, i: (p, tok_off // seq + b))],
        out_specs=pl.BlockSpec((LANES, tq), lambda b, p, i: (p, b * nq + i)),
        out_shape=jax.ShapeDtypeStruct((512, batch * seq), BF16),
        compiler_params=_params("parallel", "parallel", "parallel"),
        name="mla_attn",
    )(q, k, vt)


def _softmax_update(s, m, l):
    m_new = jnp.maximum(m, jnp.max(s, axis=-1, keepdims=True))
    a = jnp.exp(m - m_new)
    p = jnp.exp(s - m_new)
    return p, m_new, a * l + jnp.sum(p, axis=-1, keepdims=True), a


def _split_lanes(x):
    low = lax.broadcasted_iota(jnp.int32, x.shape, 1) < HEAD_DIM
    zero = jnp.zeros_like(x)
    return jnp.where(low, x, zero), jnp.where(low, zero, x)


def _na_bias(rpb):
    c = np.arange(GRID_W)
    kc = np.arange(GRID_W)
    c0 = np.clip(c - NA_WIN_W // 2, 0, GRID_W - NA_WIN_W)
    ok = (kc[None, :] >= c0[:, None]) & (kc[None, :] < c0[:, None] + NA_WIN_W)
    pad = GRID_W - NA_WIN_W
    padded = jnp.pad(rpb.astype(F32), ((0, 0), (0, 0), (pad, pad)))
    cols = jnp.stack([padded[:, :, GRID_W - 1 - ci:2 * GRID_W - 1 - ci] for ci in range(GRID_W)], axis=2)
    cols = jnp.where(ok[None, None], cols, NEG_INF)
    per_u = [cols[:, NA_WIN_H - 1 - u:2 * NA_WIN_H - 1 - u] for u in range(NA_WIN_H)]
    b = jnp.stack(per_u, axis=1)
    return jnp.transpose(b, (0, 1, 3, 2, 4)).reshape(rpb.shape[0], NA_WIN_H, GRID_W, NA_WIN_H * GRID_W)


def _na_kernel(q_ref, kp_ref, kc_ref, kn_ref, vp_ref, vc_ref, vn_ref, b_ref, o_ref, kbuf, vbuf, *, rows):
    blk = ROWS_PER_STEP * GRID_W
    win = NA_WIN_H * GRID_W
    j = pl.program_id(2)
    for n, (kr, vr) in enumerate(((kp_ref, vp_ref), (kc_ref, vc_ref), (kn_ref, vn_ref))):
        kbuf[n * blk:(n + 1) * blk, :] = kr[...]
        vbuf[n * blk:(n + 1) * blk, :] = vr[...]
    low = lax.broadcasted_iota(jnp.int32, (GRID_W, LANES), 1) < HEAD_DIM

    def row_body(i, carry):
        r = j * ROWS_PER_STEP + i
        r0 = jnp.clip(r - NA_WIN_H // 2, 0, rows - NA_WIN_H)
        u = r - r0
        off = pl.multiple_of((r0 - j * ROWS_PER_STEP + ROWS_PER_STEP) * GRID_W, GRID_W)
        ke, ko = _split_lanes(kbuf[pl.ds(off, win), :])
        ve, vo = _split_lanes(vbuf[pl.ds(off, win), :])
        qrows = pl.ds(pl.multiple_of(i * GRID_W, GRID_W), GRID_W)
        q = q_ref[qrows, :]
        neg = jnp.full((GRID_W, 1), NEG_INF, F32)
        zero = jnp.zeros((GRID_W, 1), F32)
        pe, _, le, _ = _softmax_update(_dot_nt(q, ke) + b_ref[0, u], neg, zero)
        po, _, lo, _ = _softmax_update(_dot_nt(q, ko) + b_ref[1, u], neg, zero)
        acc = _dot(pe.astype(BF16), ve) + _dot(po.astype(BF16), vo)
        o_ref[qrows, :] = (acc / jnp.where(low, le, lo)).astype(o_ref.dtype)
        return carry

    lax.fori_loop(0, ROWS_PER_STEP, row_body, 0)


def _na_attn(h, bias, *, tok_off, batch, seq):
    rows = seq // GRID_W
    blk = ROWS_PER_STEP * GRID_W
    assert rows % ROWS_PER_STEP == 0 and rows >= NA_WIN_H and tok_off % blk == 0
    nb = rows // ROWS_PER_STEP
    n_pairs = bias.shape[0] // 2
    base = tok_off // blk

    def tokmap(col0, shift):
        return lambda b, p, j: (base + b * nb + jnp.clip(j + shift, 0, nb - 1), col0 // LANES + p)

    kv_specs = [pl.BlockSpec((blk, LANES), tokmap(c0, s)) for c0 in (C_NAK, C_NAV) for s in (-1, 0, 1)]
    return pl.pallas_call(
        functools.partial(_na_kernel, rows=rows),
        grid=(batch, n_pairs, nb),
        in_specs=[pl.BlockSpec((blk, LANES), tokmap(C_NAQ, 0))] + kv_specs
                 + [pl.BlockSpec((2, NA_WIN_H, GRID_W, NA_WIN_H * GRID_W), lambda b, p, j: (p, 0, 0, 0))],
        out_specs=pl.BlockSpec((blk, LANES), lambda b, p, j: (b * nb + j, p)),
        out_shape=jax.ShapeDtypeStruct((batch * seq, 512), BF16),
        scratch_shapes=[pltpu.VMEM((3 * blk, LANES), BF16), pltpu.VMEM((3 * blk, LANES), BF16)],
        compiler_params=_params("parallel", "parallel", "parallel"),
        name="na_attn",
    )(h, h, h, h, h, h, h, bias)


def _merge_kernel(ya_ref, ybt_ref, yct_ref, g0_ref, g1_ref, g2_ref, x_ref, wa_ref, wb_ref, wc_ref, wo_ref,
                  lg_ref, lb_ref, o32_ref, o16_ref, *, alpha):
    def gate(g_ref):
        return jax.nn.sigmoid(g_ref[...].astype(F32))

    merged = (gate(g0_ref) * _dot(ya_ref[...], wa_ref[...])
              + gate(g1_ref) * _dot_tn(ybt_ref[...], wb_ref[...])
              + gate(g2_ref) * _dot_tn(yct_ref[...], wc_ref[...]))
    mix = _dot(merged.astype(BF16), wo_ref[...])
    y = _layernorm(alpha * x_ref[...] + mix, lg_ref[...], lb_ref[...])
    o32_ref[...] = y
    o16_ref[...] = y.astype(BF16)


def _merge(ya, ybt, yct, h, x, wa, wb, wc, wo, lg, lb, *, alpha, tm):
    t = x.shape[0]
    row = lambda i: (i, 0)
    col = lambda i: (0, i)
    const = lambda i: (0, 0)
    gate = lambda n: pl.BlockSpec((tm, D_MODEL), lambda i: (i, C_GATE // D_MODEL + n))
    return pl.pallas_call(
        functools.partial(_merge_kernel, alpha=alpha),
        grid=(t // tm,),
        in_specs=[pl.BlockSpec((tm, 512), row), pl.BlockSpec((512, tm), col), pl.BlockSpec((512, tm), col),
                  gate(0), gate(1), gate(2), pl.BlockSpec((tm, D_MODEL), row)]
                 + [pl.BlockSpec((512, D_MODEL), const)] * 3 + [pl.BlockSpec((D_MODEL, D_MODEL), const)]
                 + [pl.BlockSpec((1, D_MODEL), const)] * 2,
        out_specs=[pl.BlockSpec((tm, D_MODEL), row)] * 2,
        out_shape=[jax.ShapeDtypeStruct((t, D_MODEL), F32), jax.ShapeDtypeStruct((t, D_MODEL), BF16)],
        compiler_params=_params("parallel"),
        name="merge_ln1",
    )(ya, ybt, yct, h, h, h, x, wa, wb, wc, wo, lg, lb)


def _ffn_body(tv_ref, x_ref, w1_ref, w3_ref, w2_ref, xb, acc):
    i, j = pl.program_id(0), pl.program_id(1)

    @pl.when(j == 0)
    def _():
        acc[...] = jnp.zeros_like(acc)
        xb[...] = x_ref[...].astype(BF16)

    @pl.when(tv_ref[i] != 0)
    def _():
        x = xb[...]
        a = _dot(x, w1_ref[0])
        b = _dot(x, w3_ref[0])
        mid = (a * jax.nn.sigmoid(a)) * b
        acc[...] += _dot(mid.astype(BF16), w2_ref[0])


def _ffn_dense_kernel(te_ref, tv_ref, x_ref, w1_ref, w3_ref, w2_ref, r_ref, lg_ref, lb_ref, o32_ref, o16_ref,
                      xb, acc, *, alpha):
    _ffn_body(tv_ref, x_ref, w1_ref, w3_ref, w2_ref, xb, acc)

    @pl.when(pl.program_id(1) == pl.num_programs(1) - 1)
    def _():
        y = _layernorm(alpha * r_ref[...] + acc[...], lg_ref[...], lb_ref[...])
        o32_ref[...] = y
        o16_ref[...] = y.astype(BF16)


def _ffn_group_kernel(te_ref, tv_ref, x_ref, w1_ref, w3_ref, w2_ref, gw_ref, o_ref, xb, acc):
    _ffn_body(tv_ref, x_ref, w1_ref, w3_ref, w2_ref, xb, acc)

    @pl.when(pl.program_id(1) == pl.num_programs(1) - 1)
    def _():
        o_ref[...] = acc[...] * gw_ref[...]


def _ffn_specs(tm, tf, d, nf):
    fidx = lambda j, v: j * v + (nf - 1) * (1 - v)
    return [pl.BlockSpec((tm, d), lambda i, j, te, tv: (i, 0)),
            pl.BlockSpec((1, d, tf), lambda i, j, te, tv: (te[i], 0, fidx(j, tv[i]))),
            pl.BlockSpec((1, d, tf), lambda i, j, te, tv: (te[i], 0, fidx(j, tv[i]))),
            pl.BlockSpec((1, tf, d), lambda i, j, te, tv: (te[i], fidx(j, tv[i]), 0))]


def _ffn_dense(x16, x32, w1, w3, w2, lg, lb, *, alpha, tm, tf):
    t, d = x16.shape
    nt, nf = t // tm, w1.shape[2] // tf
    row = lambda i, j, te, tv: (i, 0)
    const = lambda i, j, te, tv: (0, 0)
    te = jnp.zeros((nt,), jnp.int32)
    tv = jnp.ones((nt,), jnp.int32)
    return pl.pallas_call(
        functools.partial(_ffn_dense_kernel, alpha=alpha),
        grid_spec=pltpu.PrefetchScalarGridSpec(
            num_scalar_prefetch=2, grid=(nt, nf),
            in_specs=_ffn_specs(tm, tf, d, nf) + [pl.BlockSpec((tm, d), row), pl.BlockSpec((1, d), const),
                                                   pl.BlockSpec((1, d), const)],
            out_specs=[pl.BlockSpec((tm, d), row)] * 2,
            scratch_shapes=[pltpu.VMEM((tm, d), BF16), pltpu.VMEM((tm, d), F32)]),
        out_shape=[jax.ShapeDtypeStruct((t, d), F32), jax.ShapeDtypeStruct((t, d), BF16)],
        compiler_params=_params("parallel", "arbitrary"),
        name="ffn_dense",
    )(te, tv, x16, w1, w3, w2, x32, lg, lb)


def _ffn_grouped(xs, gw, te, tv, w1, w3, w2, *, tm, tf):
    n, d = xs.shape
    nt, nf = n // tm, w1.shape[2] // tf
    row = lambda i, j, te, tv: (i, 0)
    return pl.pallas_call(
        _ffn_group_kernel,
        grid_spec=pltpu.PrefetchScalarGridSpec(
            num_scalar_prefetch=2, grid=(nt, nf),
            in_specs=_ffn_specs(tm, tf, d, nf) + [pl.BlockSpec((tm, 1), row)],
            out_specs=pl.BlockSpec((tm, d), row),
            scratch_shapes=[pltpu.VMEM((tm, d), BF16), pltpu.VMEM((tm, d), F32)]),
        out_shape=jax.ShapeDtypeStruct((n, d), F32),
        compiler_params=_params("parallel", "arbitrary"),
        name="ffn_grouped",
    )(te, tv, xs, w1, w3, w2, gw)


def _router_kernel(x_ref, wh_ref, wl_ref, idx_ref, wt_ref):
    x = x_ref[...]
    xh = x.astype(BF16)
    xl = (x - xh.astype(F32)).astype(BF16)
    logits = _dot(xh, wh_ref[...]) + _dot(xl, wh_ref[...]) + _dot(xh, wl_ref[...])
    lane = lax.broadcasted_iota(jnp.int32, logits.shape, 1)
    ninf = jnp.float32(-jnp.inf)
    l1 = jnp.where(lane < N_EXPERTS, logits, ninf)
    m1 = jnp.max(l1, axis=-1, keepdims=True)
    i1 = jnp.min(jnp.where(l1 == m1, lane, LANES), axis=-1, keepdims=True)
    l2 = jnp.where(lane == i1, ninf, l1)
    m2 = jnp.max(l2, axis=-1, keepdims=True)
    i2 = jnp.min(jnp.where(l2 == m2, lane, LANES), axis=-1, keepdims=True)
    e = jnp.exp(m2 - m1)
    den = 1.0 + e
    idx_ref[...] = jnp.where(lane == 0, i1, jnp.where(lane == 1, i2, 0))
    wt_ref[...] = jnp.where(lane == 0, 1.0 / den, jnp.where(lane == 1, e / den, 0.0))


def _router(x32, wh, wl, *, tm):
    t, d = x32.shape
    row = lambda i: (i, 0)
    const = lambda i: (0, 0)
    return pl.pallas_call(
        _router_kernel,
        grid=(t // tm,),
        in_specs=[pl.BlockSpec((tm, d), row), pl.BlockSpec((d, LANES), const), pl.BlockSpec((d, LANES), const)],
        out_specs=[pl.BlockSpec((tm, LANES), row)] * 2,
        out_shape=[jax.ShapeDtypeStruct((t, LANES), jnp.int32), jax.ShapeDtypeStruct((t, LANES), F32)],
        compiler_params=_params("parallel"),
        name="router",
    )(x32, wh, wl)


def _row_copy(src_hbm, src_row, dst_ref, dst_row, sem):
    return pltpu.make_async_copy(src_hbm.at[pl.ds(src_row, 1)], dst_ref.at[pl.ds(dst_row, 1)], sem)


def _gather_kernel(src_ref, x_hbm, o_ref, sem):
    tm = o_ref.shape[0]
    base = pl.program_id(0) * tm

    def issue(r, c):
        _row_copy(x_hbm, src_ref[base + r], o_ref, r, sem).start()
        return c

    def wait(r, c):
        _row_copy(x_hbm, 0, o_ref, r, sem).wait()
        return c

    lax.fori_loop(0, tm, issue, 0, unroll=DMA_UNROLL)
    lax.fori_loop(0, tm, wait, 0, unroll=DMA_UNROLL)


def _gather_rows(x32, src, *, tm):
    n = src.shape[0]
    d = x32.shape[1]
    return pl.pallas_call(
        _gather_kernel,
        grid_spec=pltpu.PrefetchScalarGridSpec(
            num_scalar_prefetch=1, grid=(n // tm,),
            in_specs=[pl.BlockSpec(memory_space=pl.ANY)],
            out_specs=pl.BlockSpec((tm, d), lambda i, src: (i, 0)),
            scratch_shapes=[pltpu.SemaphoreType.DMA(())]),
        out_shape=jax.ShapeDtypeStruct((n, d), x32.dtype),
        compiler_params=_params("arbitrary"),
        name="moe_gather",
    )(src, x32)


def _combine_kernel(pos_ref, x_ref, ys_hbm, lg_ref, lb_ref, o32_ref, o16_ref, buf0, buf1, sem, *, alpha):
    tm = x_ref.shape[0]
    base = pl.program_id(0) * tm

    def issue(r, c):
        _row_copy(ys_hbm, pos_ref[2 * (base + r)], buf0, r, sem).start()
        _row_copy(ys_hbm, pos_ref[2 * (base + r) + 1], buf1, r, sem).start()
        return c

    def wait(r, c):
        _row_copy(ys_hbm, 0, buf0, r, sem).wait()
        _row_copy(ys_hbm, 0, buf1, r, sem).wait()
        return c

    lax.fori_loop(0, tm, issue, 0, unroll=DMA_UNROLL)
    lax.fori_loop(0, tm, wait, 0, unroll=DMA_UNROLL)
    y = _layernorm(alpha * x_ref[...] + (buf0[...] + buf1[...]), lg_ref[...], lb_ref[...])
    o32_ref[...] = y
    o16_ref[...] = y.astype(BF16)


def _combine(pos, x32, ys, lg, lb, *, alpha, tm):
    t, d = x32.shape
    row = lambda i, pos: (i, 0)
    const = lambda i, pos: (0, 0)
    return pl.pallas_call(
        functools.partial(_combine_kernel, alpha=alpha),
        grid_spec=pltpu.PrefetchScalarGridSpec(
            num_scalar_prefetch=1, grid=(t // tm,),
            in_specs=[pl.BlockSpec((tm, d), row), pl.BlockSpec(memory_space=pl.ANY),
                      pl.BlockSpec((1, d), const), pl.BlockSpec((1, d), const)],
            out_specs=[pl.BlockSpec((tm, d), row)] * 2,
            scratch_shapes=[pltpu.VMEM((tm, d), F32), pltpu.VMEM((tm, d), F32), pltpu.SemaphoreType.DMA(())]),
        out_shape=[jax.ShapeDtypeStruct((t, d), F32), jax.ShapeDtypeStruct((t, d), BF16)],
        compiler_params=_params("arbitrary"),
        name="moe_combine",
    )(pos, x32, ys, lg, lb)


def _route_meta(idx, wts, tm):
    t = idx.shape[0]
    a = 2 * t
    e = idx.reshape(a)
    w = wts.reshape(a)
    onehot = (e[:, None] == jnp.arange(N_EXPERTS, dtype=jnp.int32)[None, :]).astype(jnp.int32)
    csum = jnp.cumsum(onehot, axis=0)
    rank = jnp.sum((csum - onehot) * onehot, axis=1)
    cnt = csum[-1]
    pcnt = ((cnt + tm - 1) // tm) * tm
    pend = jnp.cumsum(pcnt)
    pos = jnp.sum(onehot * (pend - pcnt)[None, :], axis=1) + rank
    n_rows = a + N_EXPERTS * tm
    src = jnp.zeros((n_rows,), jnp.int32).at[pos].set(jnp.arange(a, dtype=jnp.int32) // 2)
    gw = jnp.zeros((n_rows,), F32).at[pos].set(w)
    start = jnp.arange(n_rows // tm, dtype=jnp.int32) * tm
    te = jnp.minimum(jnp.sum((start[:, None] >= pend[None, :]).astype(jnp.int32), axis=1), N_EXPERTS - 1)
    tv = (start < pend[-1]).astype(jnp.int32)
    return pos.astype(jnp.int32), src, gw.reshape(n_rows, 1), te.astype(jnp.int32), tv


def _prep_w_in(w):
    d = w.shape[0]
    z = lambda n: jnp.zeros((d, n), w.dtype)
    na_q, na_k, na_v = w[:, 0:512] * (HEAD_DIM ** -0.5), w[:, 512:1024], w[:, 1024:1536]
    g_q, g_k, g_v = w[:, 1536:2048], w[:, 2048:2176], w[:, 2176:2304]
    c_q, c_kv, k_r, gate = w[:, 2304:2688], w[:, 2688:2944], w[:, 2944:2976], w[:, 2976:]
    kr_blk = jnp.concatenate([z(MLA_NOPE_DIM), k_r, z(LANES - MLA_NOPE_DIM - MLA_ROPE_DIM)], axis=1)
    out = jnp.concatenate([na_q, na_k, na_v, g_q, g_k, g_v, c_kv, kr_blk, c_q, gate], axis=1).astype(BF16)
    assert out.shape[1] == H_COLS
    return out


def _prep_mla_w(w_uq, w_ukv):
    heads = w_uq.shape[1] // (MLA_NOPE_DIM + MLA_ROPE_DIM)
    wq = w_uq.reshape(MLA_Q_LORA, heads, MLA_NOPE_DIM + MLA_ROPE_DIM)
    wq = jnp.pad(wq, ((0, 0), (0, 0), (0, LANES - MLA_NOPE_DIM - MLA_ROPE_DIM))).reshape(MLA_Q_LORA, heads * LANES)
    wkv = w_ukv.reshape(MLA_KV_LORA, heads, LANES)
    wk = jnp.pad(wkv[:, :, :MLA_NOPE_DIM], ((0, 0), (0, 0), (0, LANES - MLA_NOPE_DIM))).reshape(MLA_KV_LORA, heads * LANES)
    wvt = wkv[:, :, MLA_NOPE_DIM:].reshape(MLA_KV_LORA, heads * HEAD_DIM).T
    return wq.astype(BF16), wk.astype(BF16), wvt.astype(BF16)


def _rope_tables(seq):
    pos = jnp.arange(seq, dtype=jnp.int32)
    half = 16
    inv_freq = ROPE_THETA ** (-jnp.arange(half, dtype=F32) / half)

    def cs(p):
        ang = p.astype(F32)[:, None] * inv_freq[None, :]
        return jnp.cos(ang), jnp.sin(ang)

    cr, sr = cs(pos // GRID_W)
    cc, sc = cs(pos % GRID_W)
    cp, sp = cs(pos)
    one = lambda n: jnp.ones((seq, n), F32)
    zero = lambda n: jnp.zeros((seq, n), F32)
    g_cos = jnp.tile(jnp.concatenate([cr, cr, cc, cc], axis=1), (1, 2))
    g_sin = jnp.tile(jnp.concatenate([-sr, sr, -sc, sc], axis=1), (1, 2))
    m_cos = jnp.concatenate([one(MLA_NOPE_DIM), cp, cp, one(32)], axis=1)
    m_sin = jnp.concatenate([zero(MLA_NOPE_DIM), -sp, sp, zero(32)], axis=1)
    return g_cos, g_sin, m_cos, m_sin


def kernel(x_prompt, x_sample, w_in, na_rpb, gqa_q_norm, gqa_k_norm, mla_q_norm, mla_w_uq, mla_kv_norm, mla_w_ukv, w_branch_a, w_branch_b, w_branch_c, w_out, ln1_g, ln1_b, ln2_g, ln2_b, ffn_w1, ffn_w3, ffn_w2, moe_router, moe_w1, moe_w3, moe_w2):
    depth = w_in.shape[0]
    alpha = float((2 * depth) ** 0.25)
    groups = [(x_prompt.shape[0], x_prompt.shape[1]), (x_sample.shape[0], x_sample.shape[1])]
    d = x_prompt.shape[2]
    assert d == D_MODEL
    x32 = jnp.concatenate([x_prompt.reshape(-1, d), x_sample.reshape(-1, d)], axis=0)
    t = x32.shape[0]
    x16 = x32.astype(BF16)

    tabs = [_rope_tables(s) for _, s in groups]
    g_cos, g_sin, m_cos, m_sin = [jnp.concatenate([jnp.tile(tabs[g][n], (groups[g][0], 1)) for g in range(2)], axis=0)
                                  for n in range(4)]
    blk = np.kron(np.eye(2), np.ones((HEAD_DIM, HEAD_DIM)))
    ones_bd = jnp.asarray(blk, BF16)

    tm_proj = _pick(t, 1024)
    tm_tok = _pick(t, 512, LANES)
    tm_moe = _pick(t, 512)
    row2 = lambda v: v.reshape(1, -1).astype(F32)

    for i in range(depth):
        h = _matmul(x16, _prep_w_in(w_in[i]), tm=tm_proj, tn=512)
        qg = row2(jnp.tile(gqa_q_norm[i] * (HEAD_DIM ** -0.5 * LOG2E), 2))
        kg = row2(jnp.tile(gqa_k_norm[i], 2))
        gq, gkd, gvt = _gqa_prep(h, qg, kg, g_cos, g_sin, ones_bd, tm=tm_tok)
        wq, wk, wvt = _prep_mla_w(mla_w_uq[i], mla_w_ukv[i])
        mq, mk, mvt = _mla_prep(h, row2(mla_q_norm[i]), row2(mla_kv_norm[i]), wq, wk, wvt, m_cos, m_sin, tm=tm_tok)
        bias = _na_bias(na_rpb[i])
        ya, ybt, yct = [], [], []
        off = 0
        for batch, seq in groups:
            tq = _pick(seq, 256, LANES)
            ya.append(_na_attn(h, bias, tok_off=off, batch=batch, seq=seq))
            ybt.append(_gqa_attn(gq, gkd, gvt, tok_off=off, batch=batch, seq=seq, tq=tq,
                                 tk=_pick(seq, TILES_PER_BODY // 4 * SUB_KEYS, SUB_KEYS)))
            yct.append(_mla_attn(mq, mk, mvt, tok_off=off, batch=batch, seq=seq, tq=tq,
                                 tk=_pick(seq, TILES_PER_BODY // 2 * SUB_KEYS, SUB_KEYS)))
            off += batch * seq
        ya = jnp.concatenate(ya, axis=0)
        ybt = jnp.concatenate(ybt, axis=1)
        yct = jnp.concatenate(yct, axis=1)
        x32, x16 = _merge(ya, ybt, yct, h, x32, w_branch_a[i].astype(BF16), w_branch_b[i].astype(BF16),
                          w_branch_c[i].astype(BF16), w_out[i].astype(BF16), row2(ln1_g[i]), row2(ln1_b[i]),
                          alpha=alpha, tm=tm_tok)
        j = i // 2
        if i % 2 == 0:
            f = ffn_w1.shape[2]
            x32, x16 = _ffn_dense(x16, x32, ffn_w1[j:j + 1].astype(BF16), ffn_w3[j:j + 1].astype(BF16),
                                  ffn_w2[j:j + 1].astype(BF16), row2(ln2_g[i]), row2(ln2_b[i]),
                                  alpha=alpha, tm=tm_tok, tf=_pick(f, 1408, LANES))
        else:
            rw = jnp.pad(moe_router[j], ((0, 0), (0, LANES - N_EXPERTS)))
            rw_hi = rw.astype(BF16)
            rw_lo = (rw - rw_hi.astype(F32)).astype(BF16)
            idx, wts = _router(x32, rw_hi, rw_lo, tm=tm_tok)
            pos, src, gw, te, tv = _route_meta(idx[:, :2], wts[:, :2], tm_moe)
            xs = _gather_rows(x32, src, tm=tm_moe)
            f = moe_w1.shape[3]
            ys = _ffn_grouped(xs, gw, te, tv, moe_w1[j].astype(BF16), moe_w3[j].astype(BF16),
                              moe_w2[j].astype(BF16), tm=tm_moe, tf=_pick(f, 896, LANES))
            x32, x16 = _combine(pos, x32, ys, row2(ln2_g[i]), row2(ln2_b[i]), alpha=alpha, tm=_pick(t, 256))

    tp = groups[0][0] * groups[0][1]
    return (x32[:tp].reshape(x_prompt.shape), x32[tp:].reshape(x_sample.shape))
```

```python
import functools
import math

import numpy as np
import jax
import jax.numpy as jnp
from jax import lax
from jax.experimental import pallas as pl
from jax.experimental.pallas import tpu as pltpu

F32 = jnp.float32
BF16 = jnp.bfloat16

D_MODEL = 1024
GRID_W = 64
HEAD_DIM = 64
NA_WIN_H = 8
NA_WIN_W = 16
MLA_Q_LORA = 384
MLA_KV_LORA = 256
MLA_NOPE_DIM = 64
MLA_ROPE_DIM = 32
N_EXPERTS = 8
ROPE_THETA = 10000.0
RMS_EPS = 1e-6
LN_EPS = 1e-5
NEG_INF = -1e30
LOG2E = math.log2(math.e)

LANES = 128
BF16_ROWS = 16
ROWS_PER_STEP = 8
DMA_UNROLL = 8
SUB_KEYS = 256
QK_AHEAD = 5
TILES_PER_BODY = 64
NA_AHEAD = 2

C_NAQ, C_NAK, C_NAV = 0, 512, 1024
C_GQ, C_GK, C_GV = 1536, 2048, 2176
C_CKV, C_KR, C_CQ, C_GATE = 2304, 2560, 2688, 3072
H_COLS = 6144

VMEM_LIMIT = 56 * 1024 * 1024


def _params(*sem):
    return pltpu.CompilerParams(dimension_semantics=sem, vmem_limit_bytes=VMEM_LIMIT)


def _pick(n, pref, mult=8):
    t = min(pref, n)
    while t > mult and (n % t or t % mult):
        t -= mult
    assert n % t == 0, (n, pref)
    return t


def _dot(a, b):
    return jnp.dot(a, b, preferred_element_type=F32)


def _dot_nt(a, b):
    return lax.dot_general(a, b, (((1,), (1,)), ((), ())), preferred_element_type=F32)


def _dot_tn(a, b):
    return lax.dot_general(a, b, (((0,), (0,)), ((), ())), preferred_element_type=F32)


def _layernorm(z, g, b):
    mu = jnp.mean(z, axis=-1, keepdims=True)
    zc = z - mu
    var = jnp.mean(zc * zc, axis=-1, keepdims=True)
    return zc * lax.rsqrt(var + LN_EPS) * g + b


def _mm_kernel(x_ref, w_ref, o_ref):
    o_ref[...] = _dot(x_ref[...], w_ref[...]).astype(o_ref.dtype)


def _matmul(x, w, *, tm, tn, out_dtype=BF16):
    m, k = x.shape
    n = w.shape[1]
    return pl.pallas_call(
        _mm_kernel,
        grid=(m // tm, n // tn),
        in_specs=[pl.BlockSpec((tm, k), lambda i, j: (i, 0)),
                  pl.BlockSpec((k, tn), lambda i, j: (0, j))],
        out_specs=pl.BlockSpec((tm, tn), lambda i, j: (i, j)),
        out_shape=jax.ShapeDtypeStruct((m, n), out_dtype),
        compiler_params=_params("parallel", "parallel"),
        name="proj_in",
    )(x, w)


def _rope_apply(y, cos, sin_signed, lane):
    partner = jnp.where(lane % 32 < 16, pltpu.roll(y, LANES - 16, 1), pltpu.roll(y, 16, 1))
    return y * cos + partner * sin_signed


def _group_sumsq(x, ones_bd):
    s = x * x
    s_hi = s.astype(BF16)
    s_lo = (s - s_hi.astype(F32)).astype(BF16)
    return _dot(s_hi, ones_bd) + _dot(s_lo, ones_bd)


def _gqa_prep_kernel(q_ref, k_ref, v_ref, qg_ref, kg_ref, c_ref, s_ref, ones_ref, qo_ref, kd_ref, vt_ref):
    tm = q_ref.shape[0]
    lane = lax.broadcasted_iota(jnp.int32, (tm, LANES), 1)
    cos, sin_signed, ones_bd = c_ref[...], s_ref[...], ones_ref[...]

    def norm_rope(x, gain):
        y = x * lax.rsqrt(_group_sumsq(x, ones_bd) * (1.0 / HEAD_DIM) + RMS_EPS) * gain
        return _rope_apply(y, cos, sin_signed, lane)

    for j in range(q_ref.shape[1] // LANES):
        sl = slice(j * LANES, (j + 1) * LANES)
        qo_ref[:, sl] = norm_rope(q_ref[:, sl].astype(F32), qg_ref[...]).astype(BF16)
    low = lane < HEAD_DIM
    k = norm_rope(k_ref[...].astype(F32), kg_ref[...])
    k_sw = pltpu.roll(k, HEAD_DIM, 1)
    kd_ref[:, 0:LANES] = jnp.where(low, k, k_sw).astype(BF16)
    kd_ref[:, LANES:2 * LANES] = jnp.where(low, k_sw, k).astype(BF16)
    vt_ref[...] = v_ref[...].astype(F32).T.astype(BF16)


def _gqa_prep(h, qg, kg, cos, sin_signed, ones_bd, *, tm):
    t = h.shape[0]
    row = lambda i: (i, 0)
    const = lambda i: (0, 0)
    return pl.pallas_call(
        _gqa_prep_kernel,
        grid=(t // tm,),
        in_specs=[pl.BlockSpec((tm, 512), lambda i: (i, C_GQ // 512)),
                  pl.BlockSpec((tm, LANES), lambda i: (i, C_GK // LANES)),
                  pl.BlockSpec((tm, LANES), lambda i: (i, C_GV // LANES)),
                  pl.BlockSpec((1, LANES), const), pl.BlockSpec((1, LANES), const),
                  pl.BlockSpec((tm, LANES), row), pl.BlockSpec((tm, LANES), row),
                  pl.BlockSpec((LANES, LANES), const)],
        out_specs=[pl.BlockSpec((tm, 512), row), pl.BlockSpec((tm, 256), row),
                   pl.BlockSpec((LANES, tm), lambda i: (0, i))],
        out_shape=[jax.ShapeDtypeStruct((t, 512), BF16), jax.ShapeDtypeStruct((t, 256), BF16),
                   jax.ShapeDtypeStruct((LANES, t), BF16)],
        compiler_params=_params("parallel"),
        name="gqa_prep",
    )(h, h, h, qg, kg, cos, sin_signed, ones_bd)


def _mla_prep_kernel(cq_ref, ckv_ref, kr_ref, qn_ref, kvn_ref, wq_ref, wk_ref, wvt_ref, c_ref, s_ref,
                     q_ref, k_ref, vt_ref, *, scale):
    tm = cq_ref.shape[0]
    lane = lax.broadcasted_iota(jnp.int32, (tm, LANES), 1)
    cos, sin_signed = c_ref[...], s_ref[...]

    def rms(x, g):
        return x * lax.rsqrt(jnp.mean(x * x, axis=-1, keepdims=True) + RMS_EPS) * g

    cq = rms(cq_ref[...].astype(F32), qn_ref[...]).astype(BF16)
    ckv = rms(ckv_ref[...].astype(F32), kvn_ref[...]).astype(BF16)
    q = _dot(cq, wq_ref[...])
    k = _dot(ckv, wk_ref[...])
    vt_ref[...] = _dot_nt(wvt_ref[...], ckv).astype(BF16)
    k_rope = _rope_apply(kr_ref[...].astype(F32), cos, sin_signed, lane)
    for hd in range(q.shape[1] // LANES):
        sl = slice(hd * LANES, (hd + 1) * LANES)
        q_ref[:, sl] = (_rope_apply(q[:, sl], cos, sin_signed, lane) * scale).astype(BF16)
        k_ref[:, sl] = (k[:, sl] + k_rope).astype(BF16)


def _mla_prep(h, qn, kvn, wq, wk, wvt, cos, sin_signed, *, tm):
    t = h.shape[0]
    row = lambda i: (i, 0)
    const = lambda i: (0, 0)
    scale = float((MLA_NOPE_DIM + MLA_ROPE_DIM) ** -0.5 * LOG2E)
    return pl.pallas_call(
        functools.partial(_mla_prep_kernel, scale=scale),
        grid=(t // tm,),
        in_specs=[pl.BlockSpec((tm, MLA_Q_LORA), lambda i: (i, C_CQ // MLA_Q_LORA)),
                  pl.BlockSpec((tm, MLA_KV_LORA), lambda i: (i, C_CKV // MLA_KV_LORA)),
                  pl.BlockSpec((tm, LANES), lambda i: (i, C_KR // LANES)),
                  pl.BlockSpec((1, MLA_Q_LORA), const), pl.BlockSpec((1, MLA_KV_LORA), const),
                  pl.BlockSpec(wq.shape, const), pl.BlockSpec(wk.shape, const), pl.BlockSpec(wvt.shape, const),
                  pl.BlockSpec((tm, LANES), row), pl.BlockSpec((tm, LANES), row)],
        out_specs=[pl.BlockSpec((tm, 1024), row), pl.BlockSpec((tm, 1024), row),
                   pl.BlockSpec((512, tm), lambda i: (0, i))],
        out_shape=[jax.ShapeDtypeStruct((t, 1024), BF16), jax.ShapeDtypeStruct((t, 1024), BF16),
                   jax.ShapeDtypeStruct((512, t), BF16)],
        compiler_params=_params("parallel"),
        name="mla_prep",
    )(h, h, h, qn, kvn, wq, wk, wvt, cos, sin_signed)


def _flash_t(q_ops, kv_fn, n_chunks, n_sub, tq):
    init = tuple((jnp.full((1, tq), NEG_INF, F32), jnp.zeros((HEAD_DIM + BF16_ROWS, tq), F32)) for _ in q_ops)

    def body(c, carry):
        carry = list(carry)
        tiles = [(s, hd) for s in range(n_sub) for hd in range(len(q_ops))]
        kvs = {}

        def scores(s, hd):
            if s not in kvs:
                kvs[s] = kv_fn(c, s)
            return _dot_nt(kvs[s][hd][0], q_ops[hd])

        pending = [scores(*tl) for tl in tiles[:QK_AHEAD]]
        for n, (s, hd) in enumerate(tiles):
            if n + QK_AHEAD < len(tiles):
                pending.append(scores(*tiles[n + QK_AHEAD]))
            st = pending.pop(0)
            m, acc = carry[hd]
            m_new = jnp.maximum(m, jnp.max(st, axis=0, keepdims=True))
            alpha = jnp.exp2(m - m_new)
            pt = jnp.exp2(st - m_new).astype(BF16)
            carry[hd] = (m_new, acc * alpha + _dot(kvs[s][hd][1], pt))
        return tuple(carry)

    res = lax.fori_loop(0, n_chunks, body, init)
    return [acc[0:HEAD_DIM] / acc[HEAD_DIM:HEAD_DIM + 1] for _, acc in res]


def _gqa_attn_kernel(q_ref, kd_ref, vt_ref, o_ref, *, tk):
    tq = q_ref.shape[0]
    n_heads = q_ref.shape[1] // HEAD_DIM
    low = lax.broadcasted_iota(jnp.int32, (tq, LANES), 1) < HEAD_DIM
    q_ops = []
    for hd in range(n_heads):
        blk = q_ref[:, (hd // 2) * LANES:(hd // 2 + 1) * LANES]
        zero = jnp.zeros_like(blk)
        q_ops.append(jnp.where(low, blk, zero) if hd % 2 == 0 else jnp.where(low, zero, blk))
    ones = jnp.ones((BF16_ROWS, SUB_KEYS), BF16)

    def kv_fn(c, s):
        start = pl.multiple_of(c * tk + s * SUB_KEYS, SUB_KEYS)
        k = kd_ref[pl.ds(start, SUB_KEYS), :]
        v_aug = jnp.concatenate([vt_ref[:, pl.ds(start, SUB_KEYS)], ones], axis=0)
        return [(k, v_aug)] * n_heads

    outs = _flash_t(q_ops, kv_fn, kd_ref.shape[0] // tk, tk // SUB_KEYS, tq)
    for hd, o in enumerate(outs):
        o_ref[hd * HEAD_DIM:(hd + 1) * HEAD_DIM, :] = o.astype(o_ref.dtype)


def _gqa_attn(q, kd, vt, *, tok_off, batch, seq, tq, tk):
    assert tok_off % seq == 0 and seq % tq == 0 and seq % tk == 0
    nq = seq // tq
    return pl.pallas_call(
        functools.partial(_gqa_attn_kernel, tk=tk),
        grid=(batch, 2, nq),
        in_specs=[pl.BlockSpec((tq, 256), lambda b, g, i: (tok_off // tq + b * nq + i, g)),
                  pl.BlockSpec((seq, LANES), lambda b, g, i: (tok_off // seq + b, g)),
                  pl.BlockSpec((HEAD_DIM, seq), lambda b, g, i: (g, tok_off // seq + b))],
        out_specs=pl.BlockSpec((256, tq), lambda b, g, i: (g, b * nq + i)),
        out_shape=jax.ShapeDtypeStruct((512, batch * seq), BF16),
        compiler_params=_params("parallel", "parallel", "parallel"),
        name="gqa_attn",
    )(q, kd, vt)


def _mla_attn_kernel(q_ref, k_ref, vt_ref, o_ref, *, tk):
    tq = q_ref.shape[0]
    n_heads = q_ref.shape[1] // LANES
    q_ops = [q_ref[:, hd * LANES:(hd + 1) * LANES] for hd in range(n_heads)]
    ones = jnp.ones((BF16_ROWS, SUB_KEYS), BF16)

    def kv_fn(c, s):
        rows = pl.ds(pl.multiple_of(c * tk + s * SUB_KEYS, SUB_KEYS), SUB_KEYS)
        return [(k_ref[rows, hd * LANES:(hd + 1) * LANES],
                 jnp.concatenate([vt_ref[hd * HEAD_DIM:(hd + 1) * HEAD_DIM, rows], ones], axis=0))
                for hd in range(n_heads)]

    outs = _flash_t(q_ops, kv_fn, k_ref.shape[0] // tk, tk // SUB_KEYS, tq)
    for hd, o in enumerate(outs):
        o_ref[hd * HEAD_DIM:(hd + 1) * HEAD_DIM, :] = o.astype(o_ref.dtype)


def _mla_attn(q, k, vt, *, tok_off, batch, seq, tq, tk):
    assert tok_off % seq == 0 and seq % tq == 0 and seq % tk == 0
    nq = seq // tq
    n_pairs = vt.shape[0] // LANES
    return pl.pallas_call(
        functools.partial(_mla_attn_kernel, tk=tk),
        grid=(batch, n_pairs, nq),
        in_specs=[pl.BlockSpec((tq, 256), lambda b, p, i: (tok_off // tq + b * nq + i, p)),
                  pl.BlockSpec((seq, 256), lambda b, p, i: (tok_off // seq + b, p)),
                  pl.BlockSpec((LANES, seq), lambda b, p, i: (p, tok_off // seq + b))],
        out_specs=pl.BlockSpec((LANES, tq), lambda b, p, i: (p, b * nq + i)),
        out_shape=jax.ShapeDtypeStruct((512, batch * seq), BF16),
        compiler_params=_params("parallel", "parallel", "parallel"),
        name="mla_attn",
    )(q, k, vt)


def _na_bias(rpb):
    c = np.arange(GRID_W)
    kc = np.arange(GRID_W)
    c0 = np.clip(c - NA_WIN_W // 2, 0, GRID_W - NA_WIN_W)
    ok = (kc[None, :] >= c0[:, None]) & (kc[None, :] < c0[:, None] + NA_WIN_W)
    pad = GRID_W - NA_WIN_W
    padded = jnp.pad(rpb.astype(F32), ((0, 0), (0, 0), (pad, pad)))
    cols = jnp.stack([padded[:, :, GRID_W - 1 - ci:2 * GRID_W - 1 - ci] for ci in range(GRID_W)], axis=2)
    cols = jnp.where(ok[None, None], cols, NEG_INF)
    per_u = [cols[:, NA_WIN_H - 1 - u:2 * NA_WIN_H - 1 - u] for u in range(NA_WIN_H)]
    b = jnp.stack(per_u, axis=1)
    return jnp.transpose(b, (0, 1, 3, 2, 4)).reshape(rpb.shape[0], NA_WIN_H, GRID_W, NA_WIN_H * GRID_W)


NA_PAIR_WIN = (NA_WIN_H + 2) * GRID_W
NA_VARIANTS = ((0, 1, 0), (2, 3, 0), (4, 4, 1), (4, 5, 0), (6, 7, 0))


def _na_bias_t(rpb):
    b = jnp.swapaxes(_na_bias(rpb), 2, 3)
    n_pairs = rpb.shape[0] // 2
    neg = lambda n: jnp.full((n * GRID_W, GRID_W), NEG_INF, F32)

    def col_block(hd, u, d):
        return jnp.concatenate([neg(d), b[hd, u], neg(2 - d)], axis=0)

    return jnp.stack([
        jnp.stack([jnp.concatenate([col_block(2 * p, ua, 0), col_block(2 * p + 1, ua, 0),
                                    col_block(2 * p, ub, d), col_block(2 * p + 1, ub, d)], axis=1)
                   for ua, ub, d in NA_VARIANTS])
        for p in range(n_pairs)])


def _na_kernel(q_ref, kp_ref, kc_ref, kn_ref, vp_ref, vc_ref, vn_ref, b_ref, o_ref, kbuf, vtbuf, *, rows):
    blk = ROWS_PER_STEP * GRID_W
    j = pl.program_id(2)
    for n, (kr, vr) in enumerate(((kp_ref, vp_ref), (kc_ref, vc_ref), (kn_ref, vn_ref))):
        kbuf[n * blk:(n + 1) * blk, :] = kr[...]
        vtbuf[:, n * blk:(n + 1) * blk] = vr[...].astype(F32).T.astype(BF16)
    low = lax.broadcasted_iota(jnp.int32, (GRID_W, LANES), 1) < HEAD_DIM
    ones = jnp.ones((BF16_ROWS, NA_PAIR_WIN), BF16)
    n_tiles = ROWS_PER_STEP // 2

    def scores(ip):
        r = j * ROWS_PER_STEP + 2 * ip
        r0 = jnp.clip(r - NA_WIN_H // 2, 0, rows - NA_WIN_H)
        variant = jnp.where(r < NA_WIN_H // 2, r // 2,
                            jnp.where(r <= rows - NA_WIN_H + 2, 2, (r - (rows - NA_WIN_H)) // 2 + 1))
        off = pl.multiple_of((r0 - j * ROWS_PER_STEP + ROWS_PER_STEP) * GRID_W, 2 * GRID_W)
        q4 = []
        for i in (2 * ip, 2 * ip + 1):
            q = q_ref[i * GRID_W:(i + 1) * GRID_W, :]
            zero = jnp.zeros_like(q)
            q4 += [jnp.where(low, q, zero), jnp.where(low, zero, q)]
        st = _dot_nt(kbuf[pl.ds(off, NA_PAIR_WIN), :], jnp.concatenate(q4, axis=0)) + b_ref[0, variant]
        return off, st

    pending = [scores(ip) for ip in range(min(NA_AHEAD, n_tiles))]
    for ip in range(n_tiles):
        if ip + NA_AHEAD < n_tiles:
            pending.append(scores(ip + NA_AHEAD))
        off, st = pending.pop(0)
        pt = jnp.exp(st - jnp.max(st, axis=0, keepdims=True)).astype(BF16)
        v_aug = jnp.concatenate([vtbuf[:, pl.ds(off, NA_PAIR_WIN)], ones], axis=0)
        o = _dot(v_aug, pt)
        t = (o[0:LANES] / o[LANES:LANES + 1]).T
        for n in range(2):
            rows_out = slice((2 * ip + n) * GRID_W, (2 * ip + n + 1) * GRID_W)
            o_ref[rows_out, :] = jnp.where(low, t[2 * n * GRID_W:(2 * n + 1) * GRID_W],
                                           t[(2 * n + 1) * GRID_W:(2 * n + 2) * GRID_W]).astype(o_ref.dtype)


def _na_attn(h, bias, *, tok_off, batch, seq):
    rows = seq // GRID_W
    blk = ROWS_PER_STEP * GRID_W
    assert rows % ROWS_PER_STEP == 0 and rows >= NA_WIN_H and tok_off % blk == 0
    nb = rows // ROWS_PER_STEP
    n_pairs = bias.shape[0]
    base = tok_off // blk

    def tokmap(col0, shift):
        return lambda b, p, j: (base + b * nb + jnp.clip(j + shift, 0, nb - 1), col0 // LANES + p)

    kv_specs = [pl.BlockSpec((blk, LANES), tokmap(c0, s)) for c0 in (C_NAK, C_NAV) for s in (-1, 0, 1)]
    return pl.pallas_call(
        functools.partial(_na_kernel, rows=rows),
        grid=(batch, n_pairs, nb),
        in_specs=[pl.BlockSpec((blk, LANES), tokmap(C_NAQ, 0))] + kv_specs
                 + [pl.BlockSpec((1, len(NA_VARIANTS), NA_PAIR_WIN, 2 * LANES), lambda b, p, j: (p, 0, 0, 0))],
        out_specs=pl.BlockSpec((blk, LANES), lambda b, p, j: (b * nb + j, p)),
        out_shape=jax.ShapeDtypeStruct((batch * seq, 512), BF16),
        scratch_shapes=[pltpu.VMEM((3 * blk, LANES), BF16), pltpu.VMEM((LANES, 3 * blk), BF16)],
        compiler_params=_params("parallel", "parallel", "parallel"),
        name="na_attn",
    )(h, h, h, h, h, h, h, bias)


def _merge_kernel(ya_ref, ybt_ref, yct_ref, g0_ref, g1_ref, g2_ref, x_ref, wa_ref, wb_ref, wc_ref, wo_ref,
                  lg_ref, lb_ref, o32_ref, o16_ref, *, alpha):
    def gate(g_ref):
        return jax.nn.sigmoid(g_ref[...].astype(F32))

    merged = (gate(g0_ref) * _dot(ya_ref[...], wa_ref[...])
              + gate(g1_ref) * _dot_tn(ybt_ref[...], wb_ref[...])
              + gate(g2_ref) * _dot_tn(yct_ref[...], wc_ref[...]))
    mix = _dot(merged.astype(BF16), wo_ref[...])
    y = _layernorm(alpha * x_ref[...] + mix, lg_ref[...], lb_ref[...])
    o32_ref[...] = y
    o16_ref[...] = y.astype(BF16)


def _merge(ya, ybt, yct, h, x, wa, wb, wc, wo, lg, lb, *, alpha, tm):
    t = x.shape[0]
    row = lambda i: (i, 0)
    col = lambda i: (0, i)
    const = lambda i: (0, 0)
    gate = lambda n: pl.BlockSpec((tm, D_MODEL), lambda i: (i, C_GATE // D_MODEL + n))
    return pl.pallas_call(
        functools.partial(_merge_kernel, alpha=alpha),
        grid=(t // tm,),
        in_specs=[pl.BlockSpec((tm, 512), row), pl.BlockSpec((512, tm), col), pl.BlockSpec((512, tm), col),
                  gate(0), gate(1), gate(2), pl.BlockSpec((tm, D_MODEL), row)]
                 + [pl.BlockSpec((512, D_MODEL), const)] * 3 + [pl.BlockSpec((D_MODEL, D_MODEL), const)]
                 + [pl.BlockSpec((1, D_MODEL), const)] * 2,
        out_specs=[pl.BlockSpec((tm, D_MODEL), row)] * 2,
        out_shape=[jax.ShapeDtypeStruct((t, D_MODEL), F32), jax.ShapeDtypeStruct((t, D_MODEL), BF16)],
        compiler_params=_params("parallel"),
        name="merge_ln1",
    )(ya, ybt, yct, h, h, h, x, wa, wb, wc, wo, lg, lb)


def _ffn_body(tv_ref, x_ref, w1_ref, w3_ref, w2_ref, xb, acc):
    i, j = pl.program_id(0), pl.program_id(1)

    @pl.when(j == 0)
    def _():
        acc[...] = jnp.zeros_like(acc)
        xb[...] = x_ref[...].astype(BF16)

    @pl.when(tv_ref[i] != 0)
    def _():
        x = xb[...]
        a = _dot(x, w1_ref[0])
        b = _dot(x, w3_ref[0])
        mid = (a * jax.nn.sigmoid(a)) * b
        acc[...] += _dot(mid.astype(BF16), w2_ref[0])


def _ffn_dense_kernel(te_ref, tv_ref, x_ref, w1_ref, w3_ref, w2_ref, r_ref, lg_ref, lb_ref, o32_ref, o16_ref,
                      xb, acc, *, alpha):
    _ffn_body(tv_ref, x_ref, w1_ref, w3_ref, w2_ref, xb, acc)

    @pl.when(pl.program_id(1) == pl.num_programs(1) - 1)
    def _():
        y = _layernorm(alpha * r_ref[...] + acc[...], lg_ref[...], lb_ref[...])
        o32_ref[...] = y
        o16_ref[...] = y.astype(BF16)


def _ffn_group_kernel(te_ref, tv_ref, x_ref, w1_ref, w3_ref, w2_ref, o_ref, xb, acc):
    _ffn_body(tv_ref, x_ref, w1_ref, w3_ref, w2_ref, xb, acc)

    @pl.when(pl.program_id(1) == pl.num_programs(1) - 1)
    def _():
        o_ref[...] = acc[...]


def _ffn_specs(tm, tf, d, nf):
    fidx = lambda j, v: j * v + (nf - 1) * (1 - v)
    return [pl.BlockSpec((tm, d), lambda i, j, te, tv: (i, 0)),
            pl.BlockSpec((1, d, tf), lambda i, j, te, tv: (te[i], 0, fidx(j, tv[i]))),
            pl.BlockSpec((1, d, tf), lambda i, j, te, tv: (te[i], 0, fidx(j, tv[i]))),
            pl.BlockSpec((1, tf, d), lambda i, j, te, tv: (te[i], fidx(j, tv[i]), 0))]


def _ffn_dense(x16, x32, w1, w3, w2, lg, lb, *, alpha, tm, tf):
    t, d = x16.shape
    nt, nf = t // tm, w1.shape[2] // tf
    row = lambda i, j, te, tv: (i, 0)
    const = lambda i, j, te, tv: (0, 0)
    te = jnp.zeros((nt,), jnp.int32)
    tv = jnp.ones((nt,), jnp.int32)
    return pl.pallas_call(
        functools.partial(_ffn_dense_kernel, alpha=alpha),
        grid_spec=pltpu.PrefetchScalarGridSpec(
            num_scalar_prefetch=2, grid=(nt, nf),
            in_specs=_ffn_specs(tm, tf, d, nf) + [pl.BlockSpec((tm, d), row), pl.BlockSpec((1, d), const),
                                                   pl.BlockSpec((1, d), const)],
            out_specs=[pl.BlockSpec((tm, d), row)] * 2,
            scratch_shapes=[pltpu.VMEM((tm, d), BF16), pltpu.VMEM((tm, d), F32)]),
        out_shape=[jax.ShapeDtypeStruct((t, d), F32), jax.ShapeDtypeStruct((t, d), BF16)],
        compiler_params=_params("parallel", "arbitrary"),
        name="ffn_dense",
    )(te, tv, x16, w1, w3, w2, x32, lg, lb)


def _ffn_grouped(xs, te, tv, w1, w3, w2, *, tm, tf):
    n, d = xs.shape
    nt, nf = n // tm, w1.shape[2] // tf
    row = lambda i, j, te, tv: (i, 0)
    return pl.pallas_call(
        _ffn_group_kernel,
        grid_spec=pltpu.PrefetchScalarGridSpec(
            num_scalar_prefetch=2, grid=(nt, nf),
            in_specs=_ffn_specs(tm, tf, d, nf),
            out_specs=pl.BlockSpec((tm, d), row),
            scratch_shapes=[pltpu.VMEM((tm, d), BF16), pltpu.VMEM((tm, d), F32)]),
        out_shape=jax.ShapeDtypeStruct((n, d), F32),
        compiler_params=_params("parallel", "arbitrary"),
        name="ffn_grouped",
    )(te, tv, xs, w1, w3, w2)


def _router_kernel(x_ref, wh_ref, wl_ref, idx_ref, wt_ref):
    x = x_ref[...]
    xh = x.astype(BF16)
    xl = (x - xh.astype(F32)).astype(BF16)
    logits = _dot(xh, wh_ref[...]) + _dot(xl, wh_ref[...]) + _dot(xh, wl_ref[...])
    lane = lax.broadcasted_iota(jnp.int32, logits.shape, 1)
    ninf = jnp.float32(-jnp.inf)
    l1 = jnp.where(lane < N_EXPERTS, logits, ninf)
    m1 = jnp.max(l1, axis=-1, keepdims=True)
    i1 = jnp.min(jnp.where(l1 == m1, lane, LANES), axis=-1, keepdims=True)
    l2 = jnp.where(lane == i1, ninf, l1)
    m2 = jnp.max(l2, axis=-1, keepdims=True)
    i2 = jnp.min(jnp.where(l2 == m2, lane, LANES), axis=-1, keepdims=True)
    e = jnp.exp(m2 - m1)
    den = 1.0 + e
    idx_ref[...] = jnp.where(lane == 0, i1, jnp.where(lane == 1, i2, 0))
    wt_ref[...] = jnp.where(lane == 0, 1.0 / den, jnp.where(lane == 1, e / den, 0.0))


def _router(x32, wh, wl, *, tm):
    t, d = x32.shape
    row = lambda i: (i, 0)
    const = lambda i: (0, 0)
    return pl.pallas_call(
        _router_kernel,
        grid=(t // tm,),
        in_specs=[pl.BlockSpec((tm, d), row), pl.BlockSpec((d, LANES), const), pl.BlockSpec((d, LANES), const)],
        out_specs=[pl.BlockSpec((tm, LANES), row)] * 2,
        out_shape=[jax.ShapeDtypeStruct((t, LANES), jnp.int32), jax.ShapeDtypeStruct((t, LANES), F32)],
        compiler_params=_params("parallel"),
        name="router",
    )(x32, wh, wl)


def _row_copy(src_hbm, src_row, dst_ref, dst_row, sem):
    return pltpu.make_async_copy(src_hbm.at[pl.ds(src_row, 1)], dst_ref.at[pl.ds(dst_row, 1)], sem)


def _gather_kernel(src_ref, x_hbm, o_ref, sem):
    tm = o_ref.shape[0]
    base = pl.program_id(0) * tm

    def issue(r, c):
        _row_copy(x_hbm, src_ref[base + r], o_ref, r, sem).start()
        return c

    def wait(r, c):
        _row_copy(x_hbm, 0, o_ref, r, sem).wait()
        return c

    lax.fori_loop(0, tm, issue, 0, unroll=DMA_UNROLL)
    lax.fori_loop(0, tm, wait, 0, unroll=DMA_UNROLL)


def _gather_rows(x32, src, *, tm):
    n = src.shape[0]
    d = x32.shape[1]
    return pl.pallas_call(
        _gather_kernel,
        grid_spec=pltpu.PrefetchScalarGridSpec(
            num_scalar_prefetch=1, grid=(n // tm,),
            in_specs=[pl.BlockSpec(memory_space=pl.ANY)],
            out_specs=pl.BlockSpec((tm, d), lambda i, src: (i, 0)),
            scratch_shapes=[pltpu.SemaphoreType.DMA(())]),
        out_shape=jax.ShapeDtypeStruct((n, d), x32.dtype),
        compiler_params=_params("arbitrary"),
        name="moe_gather",
    )(src, x32)


def _combine_kernel(pos_ref, x_ref, wt_ref, ys_hbm, lg_ref, lb_ref, o32_ref, o16_ref, buf0, buf1, sem, *, alpha):
    tm = x_ref.shape[0]
    base = pl.program_id(0) * tm

    def issue(r, c):
        _row_copy(ys_hbm, pos_ref[2 * (base + r)], buf0, r, sem).start()
        _row_copy(ys_hbm, pos_ref[2 * (base + r) + 1], buf1, r, sem).start()
        return c

    def wait(r, c):
        _row_copy(ys_hbm, 0, buf0, r, sem).wait()
        _row_copy(ys_hbm, 0, buf1, r, sem).wait()
        return c

    lax.fori_loop(0, tm, issue, 0, unroll=DMA_UNROLL)
    lax.fori_loop(0, tm, wait, 0, unroll=DMA_UNROLL)
    wt = wt_ref[...]
    ff = wt[:, 0:1] * buf0[...] + wt[:, 1:2] * buf1[...]
    y = _layernorm(alpha * x_ref[...] + ff, lg_ref[...], lb_ref[...])
    o32_ref[...] = y
    o16_ref[...] = y.astype(BF16)


def _combine(pos, x32, wts, ys, lg, lb, *, alpha, tm):
    t, d = x32.shape
    row = lambda i, pos: (i, 0)
    const = lambda i, pos: (0, 0)
    return pl.pallas_call(
        functools.partial(_combine_kernel, alpha=alpha),
        grid_spec=pltpu.PrefetchScalarGridSpec(
            num_scalar_prefetch=1, grid=(t // tm,),
            in_specs=[pl.BlockSpec((tm, d), row), pl.BlockSpec((tm, LANES), row), pl.BlockSpec(memory_space=pl.ANY),
                      pl.BlockSpec((1, d), const), pl.BlockSpec((1, d), const)],
            out_specs=[pl.BlockSpec((tm, d), row)] * 2,
            scratch_shapes=[pltpu.VMEM((tm, d), F32), pltpu.VMEM((tm, d), F32), pltpu.SemaphoreType.DMA(())]),
        out_shape=[jax.ShapeDtypeStruct((t, d), F32), jax.ShapeDtypeStruct((t, d), BF16)],
        compiler_params=_params("arbitrary"),
        name="moe_combine",
    )(pos, x32, wts, ys, lg, lb)


def _route_meta(idx, tm):
    t = idx.shape[0]
    a = 2 * t
    e = idx.reshape(a)
    onehot = (e[:, None] == jnp.arange(N_EXPERTS, dtype=jnp.int32)[None, :]).astype(jnp.int32)
    csum = jnp.cumsum(onehot, axis=0)
    rank = jnp.sum((csum - onehot) * onehot, axis=1)
    cnt = csum[-1]
    pcnt = ((cnt + tm - 1) // tm) * tm
    pend = jnp.cumsum(pcnt)
    pos = jnp.sum(onehot * (pend - pcnt)[None, :], axis=1) + rank
    n_rows = a + N_EXPERTS * tm
    src = jnp.zeros((n_rows,), jnp.int32).at[pos].set(jnp.arange(a, dtype=jnp.int32) // 2)
    start = jnp.arange(n_rows // tm, dtype=jnp.int32) * tm
    te = jnp.minimum(jnp.sum((start[:, None] >= pend[None, :]).astype(jnp.int32), axis=1), N_EXPERTS - 1)
    tv = (start < pend[-1]).astype(jnp.int32)
    return pos.astype(jnp.int32), src, te.astype(jnp.int32), tv


def _prep_w_in(w):
    d = w.shape[0]
    z = lambda n: jnp.zeros((d, n), w.dtype)
    na_q, na_k, na_v = w[:, 0:512] * (HEAD_DIM ** -0.5), w[:, 512:1024], w[:, 1024:1536]
    g_q, g_k, g_v = w[:, 1536:2048], w[:, 2048:2176], w[:, 2176:2304]
    c_q, c_kv, k_r, gate = w[:, 2304:2688], w[:, 2688:2944], w[:, 2944:2976], w[:, 2976:]
    kr_blk = jnp.concatenate([z(MLA_NOPE_DIM), k_r, z(LANES - MLA_NOPE_DIM - MLA_ROPE_DIM)], axis=1)
    out = jnp.concatenate([na_q, na_k, na_v, g_q, g_k, g_v, c_kv, kr_blk, c_q, gate], axis=1).astype(BF16)
    assert out.shape[1] == H_COLS
    return out


def _prep_mla_w(w_uq, w_ukv):
    heads = w_uq.shape[1] // (MLA_NOPE_DIM + MLA_ROPE_DIM)
    wq = w_uq.reshape(MLA_Q_LORA, heads, MLA_NOPE_DIM + MLA_ROPE_DIM)
    wq = jnp.pad(wq, ((0, 0), (0, 0), (0, LANES - MLA_NOPE_DIM - MLA_ROPE_DIM))).reshape(MLA_Q_LORA, heads * LANES)
    wkv = w_ukv.reshape(MLA_KV_LORA, heads, LANES)
    wk = jnp.pad(wkv[:, :, :MLA_NOPE_DIM], ((0, 0), (0, 0), (0, LANES - MLA_NOPE_DIM))).reshape(MLA_KV_LORA, heads * LANES)
    wvt = wkv[:, :, MLA_NOPE_DIM:].reshape(MLA_KV_LORA, heads * HEAD_DIM).T
    return wq.astype(BF16), wk.astype(BF16), wvt.astype(BF16)


def _rope_tables(seq):
    pos = jnp.arange(seq, dtype=jnp.int32)
    half = 16
    inv_freq = ROPE_THETA ** (-jnp.arange(half, dtype=F32) / half)

    def cs(p):
        ang = p.astype(F32)[:, None] * inv_freq[None, :]
        return jnp.cos(ang), jnp.sin(ang)

    cr, sr = cs(pos // GRID_W)
    cc, sc = cs(pos % GRID_W)
    cp, sp = cs(pos)
    one = lambda n: jnp.ones((seq, n), F32)
    zero = lambda n: jnp.zeros((seq, n), F32)
    g_cos = jnp.tile(jnp.concatenate([cr, cr, cc, cc], axis=1), (1, 2))
    g_sin = jnp.tile(jnp.concatenate([-sr, sr, -sc, sc], axis=1), (1, 2))
    m_cos = jnp.concatenate([one(MLA_NOPE_DIM), cp, cp, one(32)], axis=1)
    m_sin = jnp.concatenate([zero(MLA_NOPE_DIM), -sp, sp, zero(32)], axis=1)
    return g_cos, g_sin, m_cos, m_sin


def kernel(x_prompt, x_sample, w_in, na_rpb, gqa_q_norm, gqa_k_norm, mla_q_norm, mla_w_uq, mla_kv_norm, mla_w_ukv, w_branch_a, w_branch_b, w_branch_c, w_out, ln1_g, ln1_b, ln2_g, ln2_b, ffn_w1, ffn_w3, ffn_w2, moe_router, moe_w1, moe_w3, moe_w2):
    depth = w_in.shape[0]
    alpha = float((2 * depth) ** 0.25)
    groups = [(x_prompt.shape[0], x_prompt.shape[1]), (x_sample.shape[0], x_sample.shape[1])]
    d = x_prompt.shape[2]
    assert d == D_MODEL
    x32 = jnp.concatenate([x_prompt.reshape(-1, d), x_sample.reshape(-1, d)], axis=0)
    t = x32.shape[0]
    x16 = x32.astype(BF16)

    tabs = [_rope_tables(s) for _, s in groups]
    g_cos, g_sin, m_cos, m_sin = [jnp.concatenate([jnp.tile(tabs[g][n], (groups[g][0], 1)) for g in range(2)], axis=0)
                                  for n in range(4)]
    blk = np.kron(np.eye(2), np.ones((HEAD_DIM, HEAD_DIM)))
    ones_bd = jnp.asarray(blk, BF16)

    tm_proj = _pick(t, 2048)
    tm_tok = _pick(t, 512, LANES)
    tm_moe = _pick(t, 512)
    row2 = lambda v: v.reshape(1, -1).astype(F32)

    for i in range(depth):
        h = _matmul(x16, _prep_w_in(w_in[i]), tm=tm_proj, tn=512)
        qg = row2(jnp.tile(gqa_q_norm[i] * (HEAD_DIM ** -0.5 * LOG2E), 2))
        kg = row2(jnp.tile(gqa_k_norm[i], 2))
        gq, gkd, gvt = _gqa_prep(h, qg, kg, g_cos, g_sin, ones_bd, tm=tm_tok)
        wq, wk, wvt = _prep_mla_w(mla_w_uq[i], mla_w_ukv[i])
        mq, mk, mvt = _mla_prep(h, row2(mla_q_norm[i]), row2(mla_kv_norm[i]), wq, wk, wvt, m_cos, m_sin, tm=tm_tok)
        bias = _na_bias_t(na_rpb[i])
        ya, ybt, yct = [], [], []
        off = 0
        for batch, seq in groups:
            tq = _pick(seq, 256, LANES)
            ya.append(_na_attn(h, bias, tok_off=off, batch=batch, seq=seq))
            ybt.append(_gqa_attn(gq, gkd, gvt, tok_off=off, batch=batch, seq=seq, tq=tq,
                                 tk=_pick(seq, TILES_PER_BODY // 4 * SUB_KEYS, SUB_KEYS)))
            yct.append(_mla_attn(mq, mk, mvt, tok_off=off, batch=batch, seq=seq, tq=tq,
                                 tk=_pick(seq, TILES_PER_BODY // 2 * SUB_KEYS, SUB_KEYS)))
            off += batch * seq
        ya = jnp.concatenate(ya, axis=0)
        ybt = jnp.concatenate(ybt, axis=1)
        yct = jnp.concatenate(yct, axis=1)
        x32, x16 = _merge(ya, ybt, yct, h, x32, w_branch_a[i].astype(BF16), w_branch_b[i].astype(BF16),
                          w_branch_c[i].astype(BF16), w_out[i].astype(BF16), row2(ln1_g[i]), row2(ln1_b[i]),
                          alpha=alpha, tm=tm_tok)
        j = i // 2
        if i % 2 == 0:
            f = ffn_w1.shape[2]
            x32, x16 = _ffn_dense(x16, x32, ffn_w1[j:j + 1].astype(BF16), ffn_w3[j:j + 1].astype(BF16),
                                  ffn_w2[j:j + 1].astype(BF16), row2(ln2_g[i]), row2(ln2_b[i]),
                                  alpha=alpha, tm=tm_tok, tf=_pick(f, 1408, LANES))
        else:
            rw = jnp.pad(moe_router[j], ((0, 0), (0, LANES - N_EXPERTS)))
            rw_hi = rw.astype(BF16)
            rw_lo = (rw - rw_hi.astype(F32)).astype(BF16)
            idx, wts = _router(x32, rw_hi, rw_lo, tm=tm_tok)
            pos, src, te, tv = _route_meta(idx[:, :2], tm_moe)
            xs = _gather_rows(x32, src, tm=tm_moe)
            f = moe_w1.shape[3]
            ys = _ffn_grouped(xs, te, tv, moe_w1[j].astype(BF16), moe_w3[j].astype(BF16),
                              moe_w2[j].astype(BF16), tm=tm_moe, tf=_pick(f, 896, LANES))
            x32, x16 = _combine(pos, x32, wts, ys, row2(ln2_g[i]), row2(ln2_b[i]), alpha=alpha, tm=_pick(t, 256))

    tp = groups[0][0] * groups[0][1]
    return (x32[:tp].reshape(x_prompt.shape), x32[tp:].reshape(x_sample.shape))
```

```python
import functools
import math

import numpy as np
import jax
import jax.numpy as jnp
from jax import lax
from jax.experimental import pallas as pl
from jax.experimental.pallas import tpu as pltpu

F32 = jnp.float32
BF16 = jnp.bfloat16

D_MODEL = 1024
GRID_W = 64
HEAD_DIM = 64
NA_WIN_H = 8
NA_WIN_W = 16
MLA_Q_LORA = 384
MLA_KV_LORA = 256
MLA_NOPE_DIM = 64
MLA_ROPE_DIM = 32
N_EXPERTS = 8
GQA_GROUP = 4
ROPE_THETA = 10000.0
RMS_EPS = 1e-6
LN_EPS = 1e-5
NEG_INF = -1e30
LOG2E = math.log2(math.e)

LANES = 128
BF16_ROWS = 16
ROWS_PER_STEP = 8
DMA_UNROLL = 8
SUB_KEYS = 256
QK_AHEAD = 5
TILES_PER_BODY = 64
NA_AHEAD = 2

C_NAQ, C_NAK, C_NAV = 0, 512, 1024
C_GQ, C_GK, C_GV = 1536, 2048, 2176
C_CKV, C_KR, C_CQ, C_GATE = 2304, 2560, 2688, 3072
H_COLS = 6144

VMEM_LIMIT = 56 * 1024 * 1024


def _params(*sem):
    return pltpu.CompilerParams(dimension_semantics=sem, vmem_limit_bytes=VMEM_LIMIT)


def _pick(n, pref, mult=8):
    t = min(pref, n)
    while t > mult and (n % t or t % mult):
        t -= mult
    assert n % t == 0, (n, pref)
    return t


def _dot(a, b):
    return jnp.dot(a, b, preferred_element_type=F32)


def _dot_nt(a, b):
    return lax.dot_general(a, b, (((1,), (1,)), ((), ())), preferred_element_type=F32)


def _dot_tn(a, b):
    return lax.dot_general(a, b, (((0,), (0,)), ((), ())), preferred_element_type=F32)


def _layernorm(z, g, b):
    mu = jnp.mean(z, axis=-1, keepdims=True)
    zc = z - mu
    var = jnp.mean(zc * zc, axis=-1, keepdims=True)
    return zc * lax.rsqrt(var + LN_EPS) * g + b


def _mm_kernel(x_ref, w_ref, o_ref):
    o_ref[...] = _dot(x_ref[...], w_ref[...]).astype(o_ref.dtype)


def _matmul(x, w, *, tm, tn, out_dtype=BF16):
    m, k = x.shape
    n = w.shape[1]
    return pl.pallas_call(
        _mm_kernel,
        grid=(m // tm, n // tn),
        in_specs=[pl.BlockSpec((tm, k), lambda i, j: (i, 0)),
                  pl.BlockSpec((k, tn), lambda i, j: (0, j))],
        out_specs=pl.BlockSpec((tm, tn), lambda i, j: (i, j)),
        out_shape=jax.ShapeDtypeStruct((m, n), out_dtype),
        compiler_params=_params("parallel", "parallel"),
        name="proj_in",
    )(x, w)


def _rope_apply(y, cos, sin_signed, lane):
    partner = jnp.where(lane % 32 < 16, pltpu.roll(y, LANES - 16, 1), pltpu.roll(y, 16, 1))
    return y * cos + partner * sin_signed


def _group_sumsq(x, ones_bd):
    s = x * x
    s_hi = s.astype(BF16)
    s_lo = (s - s_hi.astype(F32)).astype(BF16)
    return _dot(s_hi, ones_bd) + _dot(s_lo, ones_bd)


def _gqa_prep_kernel(q_ref, k_ref, v_ref, qg_ref, kg_ref, c_ref, s_ref, ones_ref, qo_ref, kd_ref, vt_ref):
    tm = q_ref.shape[0]
    lane = lax.broadcasted_iota(jnp.int32, (tm, LANES), 1)
    cos, sin_signed, ones_bd = c_ref[...], s_ref[...], ones_ref[...]

    def norm_rope(x, gain):
        y = x * lax.rsqrt(_group_sumsq(x, ones_bd) * (1.0 / HEAD_DIM) + RMS_EPS) * gain
        return _rope_apply(y, cos, sin_signed, lane)

    for j in range(q_ref.shape[1] // LANES):
        sl = slice(j * LANES, (j + 1) * LANES)
        qo_ref[:, sl] = norm_rope(q_ref[:, sl].astype(F32), qg_ref[...]).astype(BF16)
    low = lane < HEAD_DIM
    k = norm_rope(k_ref[...].astype(F32), kg_ref[...])
    k_sw = pltpu.roll(k, HEAD_DIM, 1)
    kd_ref[:, 0:LANES] = jnp.where(low, k, k_sw).astype(BF16)
    kd_ref[:, LANES:2 * LANES] = jnp.where(low, k_sw, k).astype(BF16)
    vt_ref[...] = v_ref[...].astype(F32).T.astype(BF16)


def _gqa_prep(h, qg, kg, cos, sin_signed, ones_bd, *, tm):
    t = h.shape[0]
    row = lambda i: (i, 0)
    const = lambda i: (0, 0)
    return pl.pallas_call(
        _gqa_prep_kernel,
        grid=(t // tm,),
        in_specs=[pl.BlockSpec((tm, 512), lambda i: (i, C_GQ // 512)),
                  pl.BlockSpec((tm, LANES), lambda i: (i, C_GK // LANES)),
                  pl.BlockSpec((tm, LANES), lambda i: (i, C_GV // LANES)),
                  pl.BlockSpec((1, LANES), const), pl.BlockSpec((1, LANES), const),
                  pl.BlockSpec((tm, LANES), row), pl.BlockSpec((tm, LANES), row),
                  pl.BlockSpec((LANES, LANES), const)],
        out_specs=[pl.BlockSpec((tm, 512), row), pl.BlockSpec((tm, 256), row),
                   pl.BlockSpec((LANES, tm), lambda i: (0, i))],
        out_shape=[jax.ShapeDtypeStruct((t, 512), BF16), jax.ShapeDtypeStruct((t, 256), BF16),
                   jax.ShapeDtypeStruct((LANES, t), BF16)],
        compiler_params=_params("parallel"),
        name="gqa_prep",
    )(h, h, h, qg, kg, cos, sin_signed, ones_bd)


def _mla_prep_kernel(cq_ref, ckv_ref, kr_ref, qn_ref, kvn_ref, wq_ref, wk_ref, wvt_ref, c_ref, s_ref,
                     q_ref, k_ref, vt_ref, *, scale):
    tm = cq_ref.shape[0]
    lane = lax.broadcasted_iota(jnp.int32, (tm, LANES), 1)
    cos, sin_signed = c_ref[...], s_ref[...]

    def rms(x, g):
        return x * lax.rsqrt(jnp.mean(x * x, axis=-1, keepdims=True) + RMS_EPS) * g

    cq = rms(cq_ref[...].astype(F32), qn_ref[...]).astype(BF16)
    ckv = rms(ckv_ref[...].astype(F32), kvn_ref[...]).astype(BF16)
    q = _dot(cq, wq_ref[...])
    k = _dot(ckv, wk_ref[...])
    vt_ref[...] = _dot_nt(wvt_ref[...], ckv).astype(BF16)
    k_rope = _rope_apply(kr_ref[...].astype(F32), cos, sin_signed, lane)
    for hd in range(q.shape[1] // LANES):
        sl = slice(hd * LANES, (hd + 1) * LANES)
        q_ref[:, sl] = (_rope_apply(q[:, sl], cos, sin_signed, lane) * scale).astype(BF16)
        k_ref[:, sl] = (k[:, sl] + k_rope).astype(BF16)


def _mla_prep(h, qn, kvn, wq, wk, wvt, cos, sin_signed, *, tm):
    t = h.shape[0]
    row = lambda i: (i, 0)
    const = lambda i: (0, 0)
    scale = float((MLA_NOPE_DIM + MLA_ROPE_DIM) ** -0.5 * LOG2E)
    return pl.pallas_call(
        functools.partial(_mla_prep_kernel, scale=scale),
        grid=(t // tm,),
        in_specs=[pl.BlockSpec((tm, MLA_Q_LORA), lambda i: (i, C_CQ // MLA_Q_LORA)),
                  pl.BlockSpec((tm, MLA_KV_LORA), lambda i: (i, C_CKV // MLA_KV_LORA)),
                  pl.BlockSpec((tm, LANES), lambda i: (i, C_KR // LANES)),
                  pl.BlockSpec((1, MLA_Q_LORA), const), pl.BlockSpec((1, MLA_KV_LORA), const),
                  pl.BlockSpec(wq.shape, const), pl.BlockSpec(wk.shape, const), pl.BlockSpec(wvt.shape, const),
                  pl.BlockSpec((tm, LANES), row), pl.BlockSpec((tm, LANES), row)],
        out_specs=[pl.BlockSpec((tm, 1024), row), pl.BlockSpec((tm, 1024), row),
                   pl.BlockSpec((512, tm), lambda i: (0, i))],
        out_shape=[jax.ShapeDtypeStruct((t, 1024), BF16), jax.ShapeDtypeStruct((t, 1024), BF16),
                   jax.ShapeDtypeStruct((512, t), BF16)],
        compiler_params=_params("parallel"),
        name="mla_prep",
    )(h, h, h, qn, kvn, wq, wk, wvt, cos, sin_signed)


def _flash_t(q_ops, kv_fn, n_chunks, n_sub, tq):
    init = tuple((jnp.full((1, tq), NEG_INF, F32), jnp.zeros((HEAD_DIM + BF16_ROWS, tq), F32)) for _ in q_ops)

    def body(c, carry):
        carry = list(carry)
        tiles = [(s, hd) for s in range(n_sub) for hd in range(len(q_ops))]
        kvs = {}

        def scores(s, hd):
            if s not in kvs:
                kvs[s] = kv_fn(c, s)
            return _dot_nt(kvs[s][hd][0], q_ops[hd])

        pending = [scores(*tl) for tl in tiles[:QK_AHEAD]]
        for n, (s, hd) in enumerate(tiles):
            if n + QK_AHEAD < len(tiles):
                pending.append(scores(*tiles[n + QK_AHEAD]))
            st = pending.pop(0)
            m, acc = carry[hd]
            m_new = jnp.maximum(m, jnp.max(st, axis=0, keepdims=True))
            alpha = jnp.exp2(m - m_new)
            pt = jnp.exp2(st - m_new).astype(BF16)
            carry[hd] = (m_new, acc * alpha + _dot(kvs[s][hd][1], pt))
        return tuple(carry)

    res = lax.fori_loop(0, n_chunks, body, init)
    return [acc[0:HEAD_DIM] / acc[HEAD_DIM:HEAD_DIM + 1] for _, acc in res]


def _gqa_attn_kernel(q_ref, kd_ref, vt_ref, o_ref, *, tk):
    tq = q_ref.shape[0]
    n_heads = q_ref.shape[1] // HEAD_DIM
    low = lax.broadcasted_iota(jnp.int32, (tq, LANES), 1) < HEAD_DIM
    q_ops = []
    for hd in range(n_heads):
        blk = q_ref[:, (hd // 2) * LANES:(hd // 2 + 1) * LANES]
        zero = jnp.zeros_like(blk)
        q_ops.append(jnp.where(low, blk, zero) if hd % 2 == 0 else jnp.where(low, zero, blk))
    ones = jnp.ones((BF16_ROWS, SUB_KEYS), BF16)

    def kv_fn(c, s):
        rows = pl.ds(pl.multiple_of(c * tk + s * SUB_KEYS, SUB_KEYS), SUB_KEYS)
        per_group = [(kd_ref[rows, g * LANES:(g + 1) * LANES],
                      jnp.concatenate([vt_ref[g * HEAD_DIM:(g + 1) * HEAD_DIM, rows], ones], axis=0))
                     for g in range(n_heads // GQA_GROUP)]
        return [per_group[hd // GQA_GROUP] for hd in range(n_heads)]

    outs = _flash_t(q_ops, kv_fn, kd_ref.shape[0] // tk, tk // SUB_KEYS, tq)
    for hd, o in enumerate(outs):
        o_ref[hd * HEAD_DIM:(hd + 1) * HEAD_DIM, :] = o.astype(o_ref.dtype)


def _gqa_attn(q, kd, vt, *, tok_off, batch, seq, tq, groups_per_step):
    n_groups = vt.shape[0] // HEAD_DIM
    gps = groups_per_step
    tk = _pick(seq, TILES_PER_BODY // (gps * GQA_GROUP) * SUB_KEYS, SUB_KEYS)
    assert tok_off % seq == 0 and seq % tq == 0 and n_groups % gps == 0
    nq = seq // tq
    return pl.pallas_call(
        functools.partial(_gqa_attn_kernel, tk=tk),
        grid=(batch, n_groups // gps, nq),
        in_specs=[pl.BlockSpec((tq, gps * GQA_GROUP * HEAD_DIM), lambda b, g, i: (tok_off // tq + b * nq + i, g)),
                  pl.BlockSpec((seq, gps * LANES), lambda b, g, i: (tok_off // seq + b, g)),
                  pl.BlockSpec((gps * HEAD_DIM, seq), lambda b, g, i: (g, tok_off // seq + b))],
        out_specs=pl.BlockSpec((gps * GQA_GROUP * HEAD_DIM, tq), lambda b, g, i: (g, b * nq + i)),
        out_shape=jax.ShapeDtypeStruct((512, batch * seq), BF16),
        compiler_params=_params("parallel", "parallel", "parallel"),
        name="gqa_attn",
    )(q, kd, vt)


def _mla_attn_kernel(q_ref, k_ref, vt_ref, o_ref, *, tk):
    tq = q_ref.shape[0]
    n_heads = q_ref.shape[1] // LANES
    q_ops = [q_ref[:, hd * LANES:(hd + 1) * LANES] for hd in range(n_heads)]
    ones = jnp.ones((BF16_ROWS, SUB_KEYS), BF16)

    def kv_fn(c, s):
        rows = pl.ds(pl.multiple_of(c * tk + s * SUB_KEYS, SUB_KEYS), SUB_KEYS)
        return [(k_ref[rows, hd * LANES:(hd + 1) * LANES],
                 jnp.concatenate([vt_ref[hd * HEAD_DIM:(hd + 1) * HEAD_DIM, rows], ones], axis=0))
                for hd in range(n_heads)]

    outs = _flash_t(q_ops, kv_fn, k_ref.shape[0] // tk, tk // SUB_KEYS, tq)
    for hd, o in enumerate(outs):
        o_ref[hd * HEAD_DIM:(hd + 1) * HEAD_DIM, :] = o.astype(o_ref.dtype)


def _mla_attn(q, k, vt, *, tok_off, batch, seq, tq, heads_per_step):
    n_heads = vt.shape[0] // HEAD_DIM
    hps = heads_per_step
    tk = _pick(seq, TILES_PER_BODY // hps * SUB_KEYS, SUB_KEYS)
    assert tok_off % seq == 0 and seq % tq == 0 and n_heads % hps == 0
    nq = seq // tq
    return pl.pallas_call(
        functools.partial(_mla_attn_kernel, tk=tk),
        grid=(batch, n_heads // hps, nq),
        in_specs=[pl.BlockSpec((tq, hps * LANES), lambda b, p, i: (tok_off // tq + b * nq + i, p)),
                  pl.BlockSpec((seq, hps * LANES), lambda b, p, i: (tok_off // seq + b, p)),
                  pl.BlockSpec((hps * HEAD_DIM, seq), lambda b, p, i: (p, tok_off // seq + b))],
        out_specs=pl.BlockSpec((hps * HEAD_DIM, tq), lambda b, p, i: (p, b * nq + i)),
        out_shape=jax.ShapeDtypeStruct((512, batch * seq), BF16),
        compiler_params=_params("parallel", "parallel", "parallel"),
        name="mla_attn",
    )(q, k, vt)


def _na_bias(rpb):
    c = np.arange(GRID_W)
    kc = np.arange(GRID_W)
    c0 = np.clip(c - NA_WIN_W // 2, 0, GRID_W - NA_WIN_W)
    ok = (kc[None, :] >= c0[:, None]) & (kc[None, :] < c0[:, None] + NA_WIN_W)
    pad = GRID_W - NA_WIN_W
    padded = jnp.pad(rpb.astype(F32), ((0, 0), (0, 0), (pad, pad)))
    cols = jnp.stack([padded[:, :, GRID_W - 1 - ci:2 * GRID_W - 1 - ci] for ci in range(GRID_W)], axis=2)
    cols = jnp.where(ok[None, None], cols, NEG_INF)
    per_u = [cols[:, NA_WIN_H - 1 - u:2 * NA_WIN_H - 1 - u] for u in range(NA_WIN_H)]
    b = jnp.stack(per_u, axis=1)
    return jnp.transpose(b, (0, 1, 3, 2, 4)).reshape(rpb.shape[0], NA_WIN_H, GRID_W, NA_WIN_H * GRID_W)


NA_PAIR_WIN = (NA_WIN_H + 2) * GRID_W
NA_VARIANTS = ((0, 1, 0), (2, 3, 0), (4, 4, 1), (4, 5, 0), (6, 7, 0))


def _na_bias_t(rpb):
    b = jnp.swapaxes(_na_bias(rpb), 2, 3)
    n_pairs = rpb.shape[0] // 2
    neg = lambda n: jnp.full((n * GRID_W, GRID_W), NEG_INF, F32)

    def col_block(hd, u, d):
        return jnp.concatenate([neg(d), b[hd, u], neg(2 - d)], axis=0)

    return jnp.stack([
        jnp.stack([jnp.concatenate([col_block(2 * p, ua, 0), col_block(2 * p + 1, ua, 0),
                                    col_block(2 * p, ub, d), col_block(2 * p + 1, ub, d)], axis=1)
                   for ua, ub, d in NA_VARIANTS])
        for p in range(n_pairs)])


def _na_kernel(q_ref, kp_ref, kc_ref, kn_ref, vp_ref, vc_ref, vn_ref, b_ref, o_ref, kbuf, vtbuf, *, rows):
    blk = ROWS_PER_STEP * GRID_W
    j = pl.program_id(2)
    for n, (kr, vr) in enumerate(((kp_ref, vp_ref), (kc_ref, vc_ref), (kn_ref, vn_ref))):
        kbuf[n * blk:(n + 1) * blk, :] = kr[...]
        vtbuf[:, n * blk:(n + 1) * blk] = vr[...].astype(F32).T.astype(BF16)
    low = lax.broadcasted_iota(jnp.int32, (GRID_W, LANES), 1) < HEAD_DIM
    ones = jnp.ones((BF16_ROWS, NA_PAIR_WIN), BF16)
    n_tiles = ROWS_PER_STEP // 2

    def scores(ip):
        r = j * ROWS_PER_STEP + 2 * ip
        r0 = jnp.clip(r - NA_WIN_H // 2, 0, rows - NA_WIN_H)
        variant = jnp.where(r < NA_WIN_H // 2, r // 2,
                            jnp.where(r <= rows - NA_WIN_H + 2, 2, (r - (rows - NA_WIN_H)) // 2 + 1))
        off = pl.multiple_of((r0 - j * ROWS_PER_STEP + ROWS_PER_STEP) * GRID_W, 2 * GRID_W)
        q4 = []
        for i in (2 * ip, 2 * ip + 1):
            q = q_ref[i * GRID_W:(i + 1) * GRID_W, :]
            zero = jnp.zeros_like(q)
            q4 += [jnp.where(low, q, zero), jnp.where(low, zero, q)]
        st = _dot_nt(kbuf[pl.ds(off, NA_PAIR_WIN), :], jnp.concatenate(q4, axis=0)) + b_ref[0, variant]
        return off, st

    pending = [scores(ip) for ip in range(min(NA_AHEAD, n_tiles))]
    for ip in range(n_tiles):
        if ip + NA_AHEAD < n_tiles:
            pending.append(scores(ip + NA_AHEAD))
        off, st = pending.pop(0)
        pt = jnp.exp(st - jnp.max(st, axis=0, keepdims=True)).astype(BF16)
        v_aug = jnp.concatenate([vtbuf[:, pl.ds(off, NA_PAIR_WIN)], ones], axis=0)
        o = _dot(v_aug, pt)
        t = (o[0:LANES] / o[LANES:LANES + 1]).T
        for n in range(2):
            rows_out = slice((2 * ip + n) * GRID_W, (2 * ip + n + 1) * GRID_W)
            o_ref[rows_out, :] = jnp.where(low, t[2 * n * GRID_W:(2 * n + 1) * GRID_W],
                                           t[(2 * n + 1) * GRID_W:(2 * n + 2) * GRID_W]).astype(o_ref.dtype)


def _na_attn(h, bias, *, tok_off, batch, seq):
    rows = seq // GRID_W
    blk = ROWS_PER_STEP * GRID_W
    assert rows % ROWS_PER_STEP == 0 and rows >= NA_WIN_H and tok_off % blk == 0
    nb = rows // ROWS_PER_STEP
    n_pairs = bias.shape[0]
    base = tok_off // blk

    def tokmap(col0, shift):
        return lambda b, p, j: (base + b * nb + jnp.clip(j + shift, 0, nb - 1), col0 // LANES + p)

    kv_specs = [pl.BlockSpec((blk, LANES), tokmap(c0, s)) for c0 in (C_NAK, C_NAV) for s in (-1, 0, 1)]
    return pl.pallas_call(
        functools.partial(_na_kernel, rows=rows),
        grid=(batch, n_pairs, nb),
        in_specs=[pl.BlockSpec((blk, LANES), tokmap(C_NAQ, 0))] + kv_specs
                 + [pl.BlockSpec((1, len(NA_VARIANTS), NA_PAIR_WIN, 2 * LANES), lambda b, p, j: (p, 0, 0, 0))],
        out_specs=pl.BlockSpec((blk, LANES), lambda b, p, j: (b * nb + j, p)),
        out_shape=jax.ShapeDtypeStruct((batch * seq, 512), BF16),
        scratch_shapes=[pltpu.VMEM((3 * blk, LANES), BF16), pltpu.VMEM((LANES, 3 * blk), BF16)],
        compiler_params=_params("parallel", "parallel", "parallel"),
        name="na_attn",
    )(h, h, h, h, h, h, h, bias)


def _merge_kernel(ya_ref, ybt_ref, yct_ref, g0_ref, g1_ref, g2_ref, x_ref, wa_ref, wb_ref, wc_ref, wo_ref,
                  lg_ref, lb_ref, o32_ref, o16_ref, *, alpha):
    def gate(g_ref):
        return jax.nn.sigmoid(g_ref[...].astype(F32))

    merged = (gate(g0_ref) * _dot(ya_ref[...], wa_ref[...])
              + gate(g1_ref) * _dot_tn(ybt_ref[...], wb_ref[...])
              + gate(g2_ref) * _dot_tn(yct_ref[...], wc_ref[...]))
    mix = _dot(merged.astype(BF16), wo_ref[...])
    y = _layernorm(alpha * x_ref[...] + mix, lg_ref[...], lb_ref[...])
    o32_ref[...] = y
    o16_ref[...] = y.astype(BF16)


def _merge(ya, ybt, yct, h, x, wa, wb, wc, wo, lg, lb, *, alpha, tm):
    t = x.shape[0]
    row = lambda i: (i, 0)
    col = lambda i: (0, i)
    const = lambda i: (0, 0)
    gate = lambda n: pl.BlockSpec((tm, D_MODEL), lambda i: (i, C_GATE // D_MODEL + n))
    return pl.pallas_call(
        functools.partial(_merge_kernel, alpha=alpha),
        grid=(t // tm,),
        in_specs=[pl.BlockSpec((tm, 512), row), pl.BlockSpec((512, tm), col), pl.BlockSpec((512, tm), col),
                  gate(0), gate(1), gate(2), pl.BlockSpec((tm, D_MODEL), row)]
                 + [pl.BlockSpec((512, D_MODEL), const)] * 3 + [pl.BlockSpec((D_MODEL, D_MODEL), const)]
                 + [pl.BlockSpec((1, D_MODEL), const)] * 2,
        out_specs=[pl.BlockSpec((tm, D_MODEL), row)] * 2,
        out_shape=[jax.ShapeDtypeStruct((t, D_MODEL), F32), jax.ShapeDtypeStruct((t, D_MODEL), BF16)],
        compiler_params=_params("parallel"),
        name="merge_ln1",
    )(ya, ybt, yct, h, h, h, x, wa, wb, wc, wo, lg, lb)


def _ffn_body(tv_ref, x_ref, w1_ref, w3_ref, w2_ref, xb, acc):
    i, j = pl.program_id(0), pl.program_id(1)

    @pl.when(j == 0)
    def _():
        acc[...] = jnp.zeros_like(acc)
        xb[...] = x_ref[...].astype(BF16)

    @pl.when(tv_ref[i] != 0)
    def _():
        x = xb[...]
        a = _dot(x, w1_ref[0])
        b = _dot(x, w3_ref[0])
        mid = (a * jax.nn.sigmoid(a)) * b
        acc[...] += _dot(mid.astype(BF16), w2_ref[0])


def _ffn_dense_kernel(te_ref, tv_ref, x_ref, w1_ref, w3_ref, w2_ref, r_ref, lg_ref, lb_ref, o32_ref, o16_ref,
                      xb, acc, *, alpha):
    _ffn_body(tv_ref, x_ref, w1_ref, w3_ref, w2_ref, xb, acc)

    @pl.when(pl.program_id(1) == pl.num_programs(1) - 1)
    def _():
        y = _layernorm(alpha * r_ref[...] + acc[...], lg_ref[...], lb_ref[...])
        o32_ref[...] = y
        o16_ref[...] = y.astype(BF16)


def _ffn_group_kernel(te_ref, tv_ref, x_ref, w1_ref, w3_ref, w2_ref, o_ref, xb, acc):
    _ffn_body(tv_ref, x_ref, w1_ref, w3_ref, w2_ref, xb, acc)

    @pl.when(pl.program_id(1) == pl.num_programs(1) - 1)
    def _():
        o_ref[...] = acc[...]


def _ffn_specs(tm, tf, d, nf):
    fidx = lambda j, v: j * v + (nf - 1) * (1 - v)
    return [pl.BlockSpec((tm, d), lambda i, j, te, tv: (i, 0)),
            pl.BlockSpec((1, d, tf), lambda i, j, te, tv: (te[i], 0, fidx(j, tv[i]))),
            pl.BlockSpec((1, d, tf), lambda i, j, te, tv: (te[i], 0, fidx(j, tv[i]))),
            pl.BlockSpec((1, tf, d), lambda i, j, te, tv: (te[i], fidx(j, tv[i]), 0))]


def _ffn_dense(x16, x32, w1, w3, w2, lg, lb, *, alpha, tm, tf):
    t, d = x16.shape
    nt, nf = t // tm, w1.shape[2] // tf
    row = lambda i, j, te, tv: (i, 0)
    const = lambda i, j, te, tv: (0, 0)
    te = jnp.zeros((nt,), jnp.int32)
    tv = jnp.ones((nt,), jnp.int32)
    return pl.pallas_call(
        functools.partial(_ffn_dense_kernel, alpha=alpha),
        grid_spec=pltpu.PrefetchScalarGridSpec(
            num_scalar_prefetch=2, grid=(nt, nf),
            in_specs=_ffn_specs(tm, tf, d, nf) + [pl.BlockSpec((tm, d), row), pl.BlockSpec((1, d), const),
                                                   pl.BlockSpec((1, d), const)],
            out_specs=[pl.BlockSpec((tm, d), row)] * 2,
            scratch_shapes=[pltpu.VMEM((tm, d), BF16), pltpu.VMEM((tm, d), F32)]),
        out_shape=[jax.ShapeDtypeStruct((t, d), F32), jax.ShapeDtypeStruct((t, d), BF16)],
        compiler_params=_params("parallel", "arbitrary"),
        name="ffn_dense",
    )(te, tv, x16, w1, w3, w2, x32, lg, lb)


def _ffn_grouped(xs, te, tv, w1, w3, w2, *, tm, tf):
    n, d = xs.shape
    nt, nf = n // tm, w1.shape[2] // tf
    row = lambda i, j, te, tv: (i, 0)
    return pl.pallas_call(
        _ffn_group_kernel,
        grid_spec=pltpu.PrefetchScalarGridSpec(
            num_scalar_prefetch=2, grid=(nt, nf),
            in_specs=_ffn_specs(tm, tf, d, nf),
            out_specs=pl.BlockSpec((tm, d), row),
            scratch_shapes=[pltpu.VMEM((tm, d), BF16), pltpu.VMEM((tm, d), F32)]),
        out_shape=jax.ShapeDtypeStruct((n, d), F32),
        compiler_params=_params("parallel", "arbitrary"),
        name="ffn_grouped",
    )(te, tv, xs, w1, w3, w2)


def _router_kernel(x_ref, wh_ref, wl_ref, idx_ref, wt_ref):
    x = x_ref[...]
    xh = x.astype(BF16)
    xl = (x - xh.astype(F32)).astype(BF16)
    logits = _dot(xh, wh_ref[...]) + _dot(xl, wh_ref[...]) + _dot(xh, wl_ref[...])
    lane = lax.broadcasted_iota(jnp.int32, logits.shape, 1)
    ninf = jnp.float32(-jnp.inf)
    l1 = jnp.where(lane < N_EXPERTS, logits, ninf)
    m1 = jnp.max(l1, axis=-1, keepdims=True)
    i1 = jnp.min(jnp.where(l1 == m1, lane, LANES), axis=-1, keepdims=True)
    l2 = jnp.where(lane == i1, ninf, l1)
    m2 = jnp.max(l2, axis=-1, keepdims=True)
    i2 = jnp.min(jnp.where(l2 == m2, lane, LANES), axis=-1, keepdims=True)
    e = jnp.exp(m2 - m1)
    den = 1.0 + e
    idx_ref[...] = jnp.where(lane == 0, i1, jnp.where(lane == 1, i2, 0))
    wt_ref[...] = jnp.where(lane == 0, 1.0 / den, jnp.where(lane == 1, e / den, 0.0))


def _router(x32, wh, wl, *, tm):
    t, d = x32.shape
    row = lambda i: (i, 0)
    const = lambda i: (0, 0)
    return pl.pallas_call(
        _router_kernel,
        grid=(t // tm,),
        in_specs=[pl.BlockSpec((tm, d), row), pl.BlockSpec((d, LANES), const), pl.BlockSpec((d, LANES), const)],
        out_specs=[pl.BlockSpec((tm, LANES), row)] * 2,
        out_shape=[jax.ShapeDtypeStruct((t, LANES), jnp.int32), jax.ShapeDtypeStruct((t, LANES), F32)],
        compiler_params=_params("parallel"),
        name="router",
    )(x32, wh, wl)


def _row_copy(src_hbm, src_row, dst_ref, dst_row, sem):
    return pltpu.make_async_copy(src_hbm.at[pl.ds(src_row, 1)], dst_ref.at[pl.ds(dst_row, 1)], sem)


def _for_rows(n, fn):
    def group(gi, c):
        for u in range(DMA_UNROLL):
            fn(gi * DMA_UNROLL + u, u % 2)
        return c

    lax.fori_loop(0, n // DMA_UNROLL, group, 0)


def _gather_kernel(src_ref, x_hbm, o_ref, sem):
    tm = o_ref.shape[0]
    base = pl.program_id(0) * tm
    _for_rows(tm, lambda r, par: _row_copy(x_hbm, src_ref[base + r], o_ref, r, sem).start(priority=par))
    _for_rows(tm, lambda r, par: _row_copy(x_hbm, 0, o_ref, r, sem).wait())


def _gather_rows(x32, src, *, tm):
    n = src.shape[0]
    d = x32.shape[1]
    return pl.pallas_call(
        _gather_kernel,
        grid_spec=pltpu.PrefetchScalarGridSpec(
            num_scalar_prefetch=1, grid=(n // tm,),
            in_specs=[pl.BlockSpec(memory_space=pl.ANY)],
            out_specs=pl.BlockSpec((tm, d), lambda i, src: (i, 0)),
            scratch_shapes=[pltpu.SemaphoreType.DMA(())]),
        out_shape=jax.ShapeDtypeStruct((n, d), x32.dtype),
        compiler_params=_params("arbitrary"),
        name="moe_gather",
    )(src, x32)


def _combine_kernel(pos_ref, x_ref, wt_ref, ys_hbm, lg_ref, lb_ref, o32_ref, o16_ref, buf0, buf1, sem, *, alpha):
    tm = x_ref.shape[0]
    base = pl.program_id(0) * tm

    def issue(r, par):
        _row_copy(ys_hbm, pos_ref[2 * (base + r)], buf0, r, sem).start(priority=0)
        _row_copy(ys_hbm, pos_ref[2 * (base + r) + 1], buf1, r, sem).start(priority=1)

    def wait(r, par):
        _row_copy(ys_hbm, 0, buf0, r, sem).wait()
        _row_copy(ys_hbm, 0, buf1, r, sem).wait()

    _for_rows(tm, issue)
    _for_rows(tm, wait)
    wt = wt_ref[...]
    ff = wt[:, 0:1] * buf0[...] + wt[:, 1:2] * buf1[...]
    y = _layernorm(alpha * x_ref[...] + ff, lg_ref[...], lb_ref[...])
    o32_ref[...] = y
    o16_ref[...] = y.astype(BF16)


def _combine(pos, x32, wts, ys, lg, lb, *, alpha, tm):
    t, d = x32.shape
    row = lambda i, pos: (i, 0)
    const = lambda i, pos: (0, 0)
    return pl.pallas_call(
        functools.partial(_combine_kernel, alpha=alpha),
        grid_spec=pltpu.PrefetchScalarGridSpec(
            num_scalar_prefetch=1, grid=(t // tm,),
            in_specs=[pl.BlockSpec((tm, d), row), pl.BlockSpec((tm, LANES), row), pl.BlockSpec(memory_space=pl.ANY),
                      pl.BlockSpec((1, d), const), pl.BlockSpec((1, d), const)],
            out_specs=[pl.BlockSpec((tm, d), row)] * 2,
            scratch_shapes=[pltpu.VMEM((tm, d), F32), pltpu.VMEM((tm, d), F32), pltpu.SemaphoreType.DMA(())]),
        out_shape=[jax.ShapeDtypeStruct((t, d), F32), jax.ShapeDtypeStruct((t, d), BF16)],
        compiler_params=_params("arbitrary"),
        name="moe_combine",
    )(pos, x32, wts, ys, lg, lb)


def _route_meta(idx, tm):
    t = idx.shape[0]
    a = 2 * t
    e = idx.reshape(a)
    onehot = (e[:, None] == jnp.arange(N_EXPERTS, dtype=jnp.int32)[None, :]).astype(jnp.int32)
    csum = jnp.cumsum(onehot, axis=0)
    rank = jnp.sum((csum - onehot) * onehot, axis=1)
    cnt = csum[-1]
    pcnt = ((cnt + tm - 1) // tm) * tm
    pend = jnp.cumsum(pcnt)
    pos = jnp.sum(onehot * (pend - pcnt)[None, :], axis=1) + rank
    n_rows = a + N_EXPERTS * tm
    src = jnp.zeros((n_rows,), jnp.int32).at[pos].set(jnp.arange(a, dtype=jnp.int32) // 2)
    start = jnp.arange(n_rows // tm, dtype=jnp.int32) * tm
    te = jnp.minimum(jnp.sum((start[:, None] >= pend[None, :]).astype(jnp.int32), axis=1), N_EXPERTS - 1)
    tv = (start < pend[-1]).astype(jnp.int32)
    return pos.astype(jnp.int32), src, te.astype(jnp.int32), tv


def _prep_w_in(w):
    d = w.shape[0]
    z = lambda n: jnp.zeros((d, n), w.dtype)
    na_q, na_k, na_v = w[:, 0:512] * (HEAD_DIM ** -0.5), w[:, 512:1024], w[:, 1024:1536]
    g_q, g_k, g_v = w[:, 1536:2048], w[:, 2048:2176], w[:, 2176:2304]
    c_q, c_kv, k_r, gate = w[:, 2304:2688], w[:, 2688:2944], w[:, 2944:2976], w[:, 2976:]
    kr_blk = jnp.concatenate([z(MLA_NOPE_DIM), k_r, z(LANES - MLA_NOPE_DIM - MLA_ROPE_DIM)], axis=1)
    out = jnp.concatenate([na_q, na_k, na_v, g_q, g_k, g_v, c_kv, kr_blk, c_q, gate], axis=1).astype(BF16)
    assert out.shape[1] == H_COLS
    return out


def _prep_mla_w(w_uq, w_ukv):
    heads = w_uq.shape[1] // (MLA_NOPE_DIM + MLA_ROPE_DIM)
    wq = w_uq.reshape(MLA_Q_LORA, heads, MLA_NOPE_DIM + MLA_ROPE_DIM)
    wq = jnp.pad(wq, ((0, 0), (0, 0), (0, LANES - MLA_NOPE_DIM - MLA_ROPE_DIM))).reshape(MLA_Q_LORA, heads * LANES)
    wkv = w_ukv.reshape(MLA_KV_LORA, heads, LANES)
    wk = jnp.pad(wkv[:, :, :MLA_NOPE_DIM], ((0, 0), (0, 0), (0, LANES - MLA_NOPE_DIM))).reshape(MLA_KV_LORA, heads * LANES)
    wvt = wkv[:, :, MLA_NOPE_DIM:].reshape(MLA_KV_LORA, heads * HEAD_DIM).T
    return wq.astype(BF16), wk.astype(BF16), wvt.astype(BF16)


def _rope_tables(seq):
    pos = jnp.arange(seq, dtype=jnp.int32)
    half = 16
    inv_freq = ROPE_THETA ** (-jnp.arange(half, dtype=F32) / half)

    def cs(p):
        ang = p.astype(F32)[:, None] * inv_freq[None, :]
        return jnp.cos(ang), jnp.sin(ang)

    cr, sr = cs(pos // GRID_W)
    cc, sc = cs(pos % GRID_W)
    cp, sp = cs(pos)
    one = lambda n: jnp.ones((seq, n), F32)
    zero = lambda n: jnp.zeros((seq, n), F32)
    g_cos = jnp.tile(jnp.concatenate([cr, cr, cc, cc], axis=1), (1, 2))
    g_sin = jnp.tile(jnp.concatenate([-sr, sr, -sc, sc], axis=1), (1, 2))
    m_cos = jnp.concatenate([one(MLA_NOPE_DIM), cp, cp, one(32)], axis=1)
    m_sin = jnp.concatenate([zero(MLA_NOPE_DIM), -sp, sp, zero(32)], axis=1)
    return g_cos, g_sin, m_cos, m_sin


def kernel(x_prompt, x_sample, w_in, na_rpb, gqa_q_norm, gqa_k_norm, mla_q_norm, mla_w_uq, mla_kv_norm, mla_w_ukv, w_branch_a, w_branch_b, w_branch_c, w_out, ln1_g, ln1_b, ln2_g, ln2_b, ffn_w1, ffn_w3, ffn_w2, moe_router, moe_w1, moe_w3, moe_w2):
    depth = w_in.shape[0]
    alpha = float((2 * depth) ** 0.25)
    groups = [(x_prompt.shape[0], x_prompt.shape[1]), (x_sample.shape[0], x_sample.shape[1])]
    d = x_prompt.shape[2]
    assert d == D_MODEL
    x32 = jnp.concatenate([x_prompt.reshape(-1, d), x_sample.reshape(-1, d)], axis=0)
    t = x32.shape[0]
    x16 = x32.astype(BF16)

    tabs = [_rope_tables(s) for _, s in groups]
    g_cos, g_sin, m_cos, m_sin = [jnp.concatenate([jnp.tile(tabs[g][n], (groups[g][0], 1)) for g in range(2)], axis=0)
                                  for n in range(4)]
    blk = np.kron(np.eye(2), np.ones((HEAD_DIM, HEAD_DIM)))
    ones_bd = jnp.asarray(blk, BF16)

    tm_proj = _pick(t, 2048)
    tm_tok = _pick(t, 512, LANES)
    tm_moe = _pick(t, 512)
    row2 = lambda v: v.reshape(1, -1).astype(F32)

    for i in range(depth):
        h = _matmul(x16, _prep_w_in(w_in[i]), tm=tm_proj, tn=512)
        qg = row2(jnp.tile(gqa_q_norm[i] * (HEAD_DIM ** -0.5 * LOG2E), 2))
        kg = row2(jnp.tile(gqa_k_norm[i], 2))
        gq, gkd, gvt = _gqa_prep(h, qg, kg, g_cos, g_sin, ones_bd, tm=tm_tok)
        wq, wk, wvt = _prep_mla_w(mla_w_uq[i], mla_w_ukv[i])
        mq, mk, mvt = _mla_prep(h, row2(mla_q_norm[i]), row2(mla_kv_norm[i]), wq, wk, wvt, m_cos, m_sin, tm=tm_tok)
        bias = _na_bias_t(na_rpb[i])
        ya, ybt, yct = [], [], []
        off = 0
        for batch, seq in groups:
            tq = _pick(seq, 256, LANES)
            ya.append(_na_attn(h, bias, tok_off=off, batch=batch, seq=seq))
            n_sub = seq // SUB_KEYS
            ybt.append(_gqa_attn(gq, gkd, gvt, tok_off=off, batch=batch, seq=seq, tq=tq,
                                 groups_per_step=2 if GQA_GROUP * n_sub < TILES_PER_BODY else 1))
            yct.append(_mla_attn(mq, mk, mvt, tok_off=off, batch=batch, seq=seq, tq=tq,
                                 heads_per_step=min(8, max(2, TILES_PER_BODY // n_sub))))
            off += batch * seq
        ya = jnp.concatenate(ya, axis=0)
        ybt = jnp.concatenate(ybt, axis=1)
        yct = jnp.concatenate(yct, axis=1)
        x32, x16 = _merge(ya, ybt, yct, h, x32, w_branch_a[i].astype(BF16), w_branch_b[i].astype(BF16),
                          w_branch_c[i].astype(BF16), w_out[i].astype(BF16), row2(ln1_g[i]), row2(ln1_b[i]),
                          alpha=alpha, tm=tm_tok)
        j = i // 2
        if i % 2 == 0:
            f = ffn_w1.shape[2]
            x32, x16 = _ffn_dense(x16, x32, ffn_w1[j:j + 1].astype(BF16), ffn_w3[j:j + 1].astype(BF16),
                                  ffn_w2[j:j + 1].astype(BF16), row2(ln2_g[i]), row2(ln2_b[i]),
                                  alpha=alpha, tm=_pick(t, 256, LANES), tf=_pick(f, 2816, 2 * LANES))
        else:
            rw = jnp.pad(moe_router[j], ((0, 0), (0, LANES - N_EXPERTS)))
            rw_hi = rw.astype(BF16)
            rw_lo = (rw - rw_hi.astype(F32)).astype(BF16)
            idx, wts = _router(x32, rw_hi, rw_lo, tm=tm_tok)
            pos, src, te, tv = _route_meta(idx[:, :2], tm_moe)
            xs = _gather_rows(x32, src, tm=tm_moe)
            f = moe_w1.shape[3]
            ys = _ffn_grouped(xs, te, tv, moe_w1[j].astype(BF16), moe_w3[j].astype(BF16),
                              moe_w2[j].astype(BF16), tm=tm_moe, tf=_pick(f, 1792, 2 * LANES))
            x32, x16 = _combine(pos, x32, wts, ys, row2(ln2_g[i]), row2(ln2_b[i]), alpha=alpha, tm=_pick(t, 256))

    tp = groups[0][0] * groups[0][1]
    return (x32[:tp].reshape(x_prompt.shape), x32[tp:].reshape(x_sample.shape))
```

```python
import functools
import math

import numpy as np
import jax
import jax.numpy as jnp
from jax import lax
from jax.experimental import pallas as pl
from jax.experimental.pallas import tpu as pltpu

F32 = jnp.float32
BF16 = jnp.bfloat16

D_MODEL = 1024
GRID_W = 64
HEAD_DIM = 64
NA_WIN_H = 8
NA_WIN_W = 16
MLA_Q_LORA = 384
MLA_KV_LORA = 256
MLA_NOPE_DIM = 64
MLA_ROPE_DIM = 32
N_EXPERTS = 8
GQA_GROUP = 4
ROPE_THETA = 10000.0
RMS_EPS = 1e-6
LN_EPS = 1e-5
NEG_INF = -1e30
LOG2E = math.log2(math.e)

LANES = 128
BF16_ROWS = 16
ROWS_PER_STEP = 8
DMA_UNROLL = 8
SUB_KEYS = 256
QK_AHEAD = 6
TILES_PER_BODY = 64
NA_AHEAD = 2

C_NAQ, C_NAK, C_NAV = 0, 512, 1024
C_GQ, C_GK, C_GV = 1536, 2048, 2176
C_CKV, C_KR, C_CQ, C_GATE = 2304, 2560, 2688, 3072
H_COLS = 6144

VMEM_LIMIT = 56 * 1024 * 1024


def _params(*sem):
    return pltpu.CompilerParams(dimension_semantics=sem, vmem_limit_bytes=VMEM_LIMIT)


def _pick(n, pref, mult=8):
    t = min(pref, n)
    while t > mult and (n % t or t % mult):
        t -= mult
    assert n % t == 0, (n, pref)
    return t


def _dot(a, b):
    return jnp.dot(a, b, preferred_element_type=F32)


def _dot_nt(a, b):
    return lax.dot_general(a, b, (((1,), (1,)), ((), ())), preferred_element_type=F32)


def _dot_tn(a, b):
    return lax.dot_general(a, b, (((0,), (0,)), ((), ())), preferred_element_type=F32)


def _layernorm(z, g, b):
    mu = jnp.mean(z, axis=-1, keepdims=True)
    zc = z - mu
    var = jnp.mean(zc * zc, axis=-1, keepdims=True)
    return zc * lax.rsqrt(var + LN_EPS) * g + b


def _mm_kernel(x_ref, w_ref, o_ref):
    o_ref[...] = _dot(x_ref[...], w_ref[...]).astype(o_ref.dtype)


def _matmul(x, w, *, tm, tn, out_dtype=BF16):
    m, k = x.shape
    n = w.shape[1]
    return pl.pallas_call(
        _mm_kernel,
        grid=(m // tm, n // tn),
        in_specs=[pl.BlockSpec((tm, k), lambda i, j: (i, 0)),
                  pl.BlockSpec((k, tn), lambda i, j: (0, j))],
        out_specs=pl.BlockSpec((tm, tn), lambda i, j: (i, j)),
        out_shape=jax.ShapeDtypeStruct((m, n), out_dtype),
        compiler_params=_params("parallel", "parallel"),
        name="proj_in",
    )(x, w)


def _rope_apply(y, cos, sin_signed, lane):
    partner = jnp.where(lane % 32 < 16, pltpu.roll(y, LANES - 16, 1), pltpu.roll(y, 16, 1))
    return y * cos + partner * sin_signed


def _group_sumsq(x, ones_bd):
    s = x * x
    s_hi = s.astype(BF16)
    s_lo = (s - s_hi.astype(F32)).astype(BF16)
    return _dot(s_hi, ones_bd) + _dot(s_lo, ones_bd)


def _gqa_prep_kernel(q_ref, k_ref, v_ref, qg_ref, kg_ref, c_ref, s_ref, ones_ref, qo_ref, kd_ref, vt_ref):
    tm = q_ref.shape[0]
    lane = lax.broadcasted_iota(jnp.int32, (tm, LANES), 1)
    cos, sin_signed, ones_bd = c_ref[...], s_ref[...], ones_ref[...]

    def norm_rope(x, gain):
        y = x * lax.rsqrt(_group_sumsq(x, ones_bd) * (1.0 / HEAD_DIM) + RMS_EPS) * gain
        return _rope_apply(y, cos, sin_signed, lane)

    for j in range(q_ref.shape[1] // LANES):
        sl = slice(j * LANES, (j + 1) * LANES)
        qo_ref[sl, :] = norm_rope(q_ref[:, sl].astype(F32), qg_ref[...]).T.astype(BF16)
    low = lane < HEAD_DIM
    k = norm_rope(k_ref[...].astype(F32), kg_ref[...])
    k_sw = pltpu.roll(k, HEAD_DIM, 1)
    kd_ref[:, 0:LANES] = jnp.where(low, k, k_sw).astype(BF16)
    kd_ref[:, LANES:2 * LANES] = jnp.where(low, k_sw, k).astype(BF16)
    vt_ref[...] = v_ref[...].astype(F32).T.astype(BF16)


def _gqa_prep(h, qg, kg, cos, sin_signed, ones_bd, *, tm):
    t = h.shape[0]
    row = lambda i: (i, 0)
    const = lambda i: (0, 0)
    return pl.pallas_call(
        _gqa_prep_kernel,
        grid=(t // tm,),
        in_specs=[pl.BlockSpec((tm, 512), lambda i: (i, C_GQ // 512)),
                  pl.BlockSpec((tm, LANES), lambda i: (i, C_GK // LANES)),
                  pl.BlockSpec((tm, LANES), lambda i: (i, C_GV // LANES)),
                  pl.BlockSpec((1, LANES), const), pl.BlockSpec((1, LANES), const),
                  pl.BlockSpec((tm, LANES), row), pl.BlockSpec((tm, LANES), row),
                  pl.BlockSpec((LANES, LANES), const)],
        out_specs=[pl.BlockSpec((512, tm), lambda i: (0, i)), pl.BlockSpec((tm, 256), row),
                   pl.BlockSpec((LANES, tm), lambda i: (0, i))],
        out_shape=[jax.ShapeDtypeStruct((512, t), BF16), jax.ShapeDtypeStruct((t, 256), BF16),
                   jax.ShapeDtypeStruct((LANES, t), BF16)],
        compiler_params=_params("parallel"),
        name="gqa_prep",
    )(h, h, h, qg, kg, cos, sin_signed, ones_bd)


def _mla_prep_kernel(cq_ref, ckv_ref, kr_ref, qn_ref, kvn_ref, wq_ref, wk_ref, wvt_ref, c_ref, s_ref,
                     q_ref, k_ref, vt_ref, *, scale):
    tm = cq_ref.shape[0]
    lane = lax.broadcasted_iota(jnp.int32, (tm, LANES), 1)
    cos, sin_signed = c_ref[...], s_ref[...]

    def rms(x, g):
        return x * lax.rsqrt(jnp.mean(x * x, axis=-1, keepdims=True) + RMS_EPS) * g

    cq = rms(cq_ref[...].astype(F32), qn_ref[...]).astype(BF16)
    ckv = rms(ckv_ref[...].astype(F32), kvn_ref[...]).astype(BF16)
    q = _dot(cq, wq_ref[...])
    k = _dot(ckv, wk_ref[...])
    vt_ref[...] = _dot_nt(wvt_ref[...], ckv).astype(BF16)
    k_rope = _rope_apply(kr_ref[...].astype(F32), cos, sin_signed, lane)
    for hd in range(q.shape[1] // LANES):
        sl = slice(hd * LANES, (hd + 1) * LANES)
        q_ref[sl, :] = (_rope_apply(q[:, sl], cos, sin_signed, lane) * scale).T.astype(BF16)
        k_ref[:, sl] = (k[:, sl] + k_rope).astype(BF16)


def _mla_prep(h, qn, kvn, wq, wk, wvt, cos, sin_signed, *, tm):
    t = h.shape[0]
    row = lambda i: (i, 0)
    const = lambda i: (0, 0)
    scale = float((MLA_NOPE_DIM + MLA_ROPE_DIM) ** -0.5 * LOG2E)
    return pl.pallas_call(
        functools.partial(_mla_prep_kernel, scale=scale),
        grid=(t // tm,),
        in_specs=[pl.BlockSpec((tm, MLA_Q_LORA), lambda i: (i, C_CQ // MLA_Q_LORA)),
                  pl.BlockSpec((tm, MLA_KV_LORA), lambda i: (i, C_CKV // MLA_KV_LORA)),
                  pl.BlockSpec((tm, LANES), lambda i: (i, C_KR // LANES)),
                  pl.BlockSpec((1, MLA_Q_LORA), const), pl.BlockSpec((1, MLA_KV_LORA), const),
                  pl.BlockSpec(wq.shape, const), pl.BlockSpec(wk.shape, const), pl.BlockSpec(wvt.shape, const),
                  pl.BlockSpec((tm, LANES), row), pl.BlockSpec((tm, LANES), row)],
        out_specs=[pl.BlockSpec((1024, tm), lambda i: (0, i)), pl.BlockSpec((tm, 1024), row),
                   pl.BlockSpec((512, tm), lambda i: (0, i))],
        out_shape=[jax.ShapeDtypeStruct((1024, t), BF16), jax.ShapeDtypeStruct((t, 1024), BF16),
                   jax.ShapeDtypeStruct((512, t), BF16)],
        compiler_params=_params("parallel"),
        name="mla_prep",
    )(h, h, h, qn, kvn, wq, wk, wvt, cos, sin_signed)


def _flash_t(q_ops, kv_fn, n_chunks, n_sub, tq):
    init = tuple((jnp.full((1, tq), NEG_INF, F32), jnp.zeros((HEAD_DIM + BF16_ROWS, tq), F32)) for _ in q_ops)

    def body(c, carry):
        carry = list(carry)
        tiles = [(s, hd) for s in range(n_sub) for hd in range(len(q_ops))]
        kvs = {}

        def scores(s, hd):
            if s not in kvs:
                kvs[s] = kv_fn(c, s)
            return _dot(kvs[s][hd][0], q_ops[hd])

        pending = [scores(*tl) for tl in tiles[:QK_AHEAD]]
        for n, (s, hd) in enumerate(tiles):
            if n + QK_AHEAD < len(tiles):
                pending.append(scores(*tiles[n + QK_AHEAD]))
            st = pending.pop(0)
            m, acc = carry[hd]
            m_new = jnp.maximum(m, jnp.max(st, axis=0, keepdims=True))
            alpha = jnp.exp2(m - m_new)
            pt = jnp.exp2(st - m_new).astype(BF16)
            carry[hd] = (m_new, acc * alpha + _dot(kvs[s][hd][1], pt))
        return tuple(carry)

    res = lax.fori_loop(0, n_chunks, body, init)
    return [acc[0:HEAD_DIM] / acc[HEAD_DIM:HEAD_DIM + 1] for _, acc in res]


def _gqa_attn_kernel(q_ref, kd_ref, vt_ref, o_ref, *, tk):
    tq = q_ref.shape[1]
    n_heads = q_ref.shape[0] // HEAD_DIM
    zero = jnp.zeros((HEAD_DIM, tq), BF16)
    q_ops = [jnp.concatenate([q_ref[hd * HEAD_DIM:(hd + 1) * HEAD_DIM, :], zero], axis=0) for hd in range(n_heads)]
    ones = jnp.ones((BF16_ROWS, SUB_KEYS), BF16)

    def kv_fn(c, s):
        rows = pl.ds(pl.multiple_of(c * tk + s * SUB_KEYS, SUB_KEYS), SUB_KEYS)
        per_group = [(kd_ref[rows, g * LANES:(g + 1) * LANES],
                      jnp.concatenate([vt_ref[g * HEAD_DIM:(g + 1) * HEAD_DIM, rows], ones], axis=0))
                     for g in range(n_heads // GQA_GROUP)]
        return [per_group[hd // GQA_GROUP] for hd in range(n_heads)]

    outs = _flash_t(q_ops, kv_fn, kd_ref.shape[0] // tk, tk // SUB_KEYS, tq)
    for hd, o in enumerate(outs):
        o_ref[hd * HEAD_DIM:(hd + 1) * HEAD_DIM, :] = o.astype(o_ref.dtype)


def _gqa_attn(q, kd, vt, *, tok_off, batch, seq, tq, groups_per_step):
    n_groups = vt.shape[0] // HEAD_DIM
    gps = groups_per_step
    tk = _pick(seq, TILES_PER_BODY // (gps * GQA_GROUP) * SUB_KEYS, SUB_KEYS)
    assert tok_off % seq == 0 and seq % tq == 0 and n_groups % gps == 0
    nq = seq // tq
    return pl.pallas_call(
        functools.partial(_gqa_attn_kernel, tk=tk),
        grid=(batch, n_groups // gps, nq),
        in_specs=[pl.BlockSpec((gps * GQA_GROUP * HEAD_DIM, tq), lambda b, g, i: (g, tok_off // tq + b * nq + i)),
                  pl.BlockSpec((seq, gps * LANES), lambda b, g, i: (tok_off // seq + b, g)),
                  pl.BlockSpec((gps * HEAD_DIM, seq), lambda b, g, i: (g, tok_off // seq + b))],
        out_specs=pl.BlockSpec((gps * GQA_GROUP * HEAD_DIM, tq), lambda b, g, i: (g, b * nq + i)),
        out_shape=jax.ShapeDtypeStruct((512, batch * seq), BF16),
        compiler_params=_params("parallel", "parallel", "parallel"),
        name="gqa_attn",
    )(q, kd, vt)


def _mla_attn_kernel(q_ref, k_ref, vt_ref, o_ref, *, tk):
    tq = q_ref.shape[1]
    n_heads = q_ref.shape[0] // LANES
    q_ops = [q_ref[hd * LANES:(hd + 1) * LANES, :] for hd in range(n_heads)]
    ones = jnp.ones((BF16_ROWS, SUB_KEYS), BF16)

    def kv_fn(c, s):
        rows = pl.ds(pl.multiple_of(c * tk + s * SUB_KEYS, SUB_KEYS), SUB_KEYS)
        return [(k_ref[rows, hd * LANES:(hd + 1) * LANES],
                 jnp.concatenate([vt_ref[hd * HEAD_DIM:(hd + 1) * HEAD_DIM, rows], ones], axis=0))
                for hd in range(n_heads)]

    outs = _flash_t(q_ops, kv_fn, k_ref.shape[0] // tk, tk // SUB_KEYS, tq)
    for hd, o in enumerate(outs):
        o_ref[hd * HEAD_DIM:(hd + 1) * HEAD_DIM, :] = o.astype(o_ref.dtype)


def _mla_attn(q, k, vt, *, tok_off, batch, seq, tq, heads_per_step):
    n_heads = vt.shape[0] // HEAD_DIM
    hps = heads_per_step
    tk = _pick(seq, TILES_PER_BODY // hps * SUB_KEYS, SUB_KEYS)
    assert tok_off % seq == 0 and seq % tq == 0 and n_heads % hps == 0
    nq = seq // tq
    return pl.pallas_call(
        functools.partial(_mla_attn_kernel, tk=tk),
        grid=(batch, n_heads // hps, nq),
        in_specs=[pl.BlockSpec((hps * LANES, tq), lambda b, p, i: (p, tok_off // tq + b * nq + i)),
                  pl.BlockSpec((seq, hps * LANES), lambda b, p, i: (tok_off // seq + b, p)),
                  pl.BlockSpec((hps * HEAD_DIM, seq), lambda b, p, i: (p, tok_off // seq + b))],
        out_specs=pl.BlockSpec((hps * HEAD_DIM, tq), lambda b, p, i: (p, b * nq + i)),
        out_shape=jax.ShapeDtypeStruct((512, batch * seq), BF16),
        compiler_params=_params("parallel", "parallel", "parallel"),
        name="mla_attn",
    )(q, k, vt)


NA_PAIR_WIN = (NA_WIN_H + 2) * GRID_W
NA_REL_ROWS = 2 * NA_WIN_H - 1


def _na_bias_planes(rpb):
    c = np.arange(GRID_W)
    kc = np.arange(GRID_W)
    c0 = np.clip(c - NA_WIN_W // 2, 0, GRID_W - NA_WIN_W)
    ok = (kc[:, None] >= c0[None, :]) & (kc[:, None] < c0[None, :] + NA_WIN_W)
    pad = GRID_W - NA_WIN_W
    padded = jnp.pad(rpb.astype(F32), ((0, 0), (0, 0), (pad, pad)))
    planes = jnp.stack([padded[:, :, GRID_W - 1 - ci:2 * GRID_W - 1 - ci] for ci in range(GRID_W)], axis=3)
    planes = jnp.where(ok[None, None], planes, NEG_INF)
    n_pairs = rpb.shape[0] // 2
    planes = planes.reshape(n_pairs, 2, NA_REL_ROWS, GRID_W, GRID_W)
    planes = jnp.concatenate([planes[:, 0], planes[:, 1]], axis=-1)
    return jnp.pad(planes, ((0, 0), (0, 1), (0, 0), (0, 0)), constant_values=NEG_INF)


def _na_kernel(q_ref, kp_ref, kc_ref, kn_ref, vp_ref, vc_ref, vn_ref, b_ref, o_ref, kbuf, vtbuf, *, rows):
    blk = ROWS_PER_STEP * GRID_W
    j = pl.program_id(2)
    for n, (kr, vr) in enumerate(((kp_ref, vp_ref), (kc_ref, vc_ref), (kn_ref, vn_ref))):
        kbuf[n * blk:(n + 1) * blk, :] = kr[...]
        vtbuf[:, n * blk:(n + 1) * blk] = vr[...].astype(F32).T.astype(BF16)
    low = lax.broadcasted_iota(jnp.int32, (GRID_W, LANES), 1) < HEAD_DIM
    ones = jnp.ones((BF16_ROWS, NA_PAIR_WIN), BF16)
    n_tiles = ROWS_PER_STEP // 2

    def scores(ip):
        r = j * ROWS_PER_STEP + 2 * ip
        r0 = jnp.clip(r - NA_WIN_H // 2, 0, rows - NA_WIN_H)
        r0b = jnp.clip(r + 1 - NA_WIN_H // 2, 0, rows - NA_WIN_H)
        off = pl.multiple_of((r0 - j * ROWS_PER_STEP + ROWS_PER_STEP) * GRID_W, 2 * GRID_W)
        bias_rows = []
        for t in range(NA_PAIR_WIN // GRID_W):
            ia = r0 + t - r + (NA_WIN_H - 1) if t < NA_WIN_H else NA_REL_ROWS
            tb = r0 + t - r0b
            ib = jnp.where((tb >= 0) & (tb < NA_WIN_H), r0 + t - (r + 1) + (NA_WIN_H - 1), NA_REL_ROWS)
            bias_rows.append(jnp.concatenate([b_ref[0, ia], b_ref[0, ib]], axis=1))
        bias = jnp.concatenate(bias_rows, axis=0)
        q4 = []
        for i in (2 * ip, 2 * ip + 1):
            q = q_ref[i * GRID_W:(i + 1) * GRID_W, :]
            zero = jnp.zeros_like(q)
            q4 += [jnp.where(low, q, zero), jnp.where(low, zero, q)]
        st = _dot_nt(kbuf[pl.ds(off, NA_PAIR_WIN), :], jnp.concatenate(q4, axis=0)) + bias
        return off, st

    pending = [scores(ip) for ip in range(min(NA_AHEAD, n_tiles))]
    for ip in range(n_tiles):
        if ip + NA_AHEAD < n_tiles:
            pending.append(scores(ip + NA_AHEAD))
        off, st = pending.pop(0)
        pt = jnp.exp(st - jnp.max(st, axis=0, keepdims=True)).astype(BF16)
        v_aug = jnp.concatenate([vtbuf[:, pl.ds(off, NA_PAIR_WIN)], ones], axis=0)
        o = _dot(v_aug, pt)
        t = (o[0:LANES] / o[LANES:LANES + 1]).T
        for n in range(2):
            rows_out = slice((2 * ip + n) * GRID_W, (2 * ip + n + 1) * GRID_W)
            o_ref[rows_out, :] = jnp.where(low, t[2 * n * GRID_W:(2 * n + 1) * GRID_W],
                                           t[(2 * n + 1) * GRID_W:(2 * n + 2) * GRID_W]).astype(o_ref.dtype)


def _na_attn(h, bias, *, tok_off, batch, seq):
    rows = seq // GRID_W
    blk = ROWS_PER_STEP * GRID_W
    assert rows % ROWS_PER_STEP == 0 and rows >= NA_WIN_H and tok_off % blk == 0
    nb = rows // ROWS_PER_STEP
    n_pairs = bias.shape[0]
    base = tok_off // blk

    def tokmap(col0, shift):
        return lambda b, p, j: (base + b * nb + jnp.clip(j + shift, 0, nb - 1), col0 // LANES + p)

    kv_specs = [pl.BlockSpec((blk, LANES), tokmap(c0, s)) for c0 in (C_NAK, C_NAV) for s in (-1, 0, 1)]
    return pl.pallas_call(
        functools.partial(_na_kernel, rows=rows),
        grid=(batch, n_pairs, nb),
        in_specs=[pl.BlockSpec((blk, LANES), tokmap(C_NAQ, 0))] + kv_specs
                 + [pl.BlockSpec((1, NA_REL_ROWS + 1, GRID_W, LANES), lambda b, p, j: (p, 0, 0, 0))],
        out_specs=pl.BlockSpec((blk, LANES), lambda b, p, j: (b * nb + j, p)),
        out_shape=jax.ShapeDtypeStruct((batch * seq, 512), BF16),
        scratch_shapes=[pltpu.VMEM((3 * blk, LANES), BF16), pltpu.VMEM((LANES, 3 * blk), BF16)],
        compiler_params=_params("parallel", "parallel", "parallel"),
        name="na_attn",
    )(h, h, h, h, h, h, h, bias)


def _merge_kernel(ya_ref, ybt_ref, yct_ref, g0_ref, g1_ref, g2_ref, x_ref, wa_ref, wb_ref, wc_ref, wo_ref,
                  lg_ref, lb_ref, o32_ref, o16_ref, *, alpha):
    def gate(g_ref):
        return jax.nn.sigmoid(g_ref[...].astype(F32))

    merged = (gate(g0_ref) * _dot(ya_ref[...], wa_ref[...])
              + gate(g1_ref) * _dot_tn(ybt_ref[...], wb_ref[...])
              + gate(g2_ref) * _dot_tn(yct_ref[...], wc_ref[...]))
    mix = _dot(merged.astype(BF16), wo_ref[...])
    y = _layernorm(alpha * x_ref[...] + mix, lg_ref[...], lb_ref[...])
    o32_ref[...] = y
    o16_ref[...] = y.astype(BF16)


def _merge(ya, ybt, yct, h, x, wa, wb, wc, wo, lg, lb, *, alpha, tm):
    t = x.shape[0]
    row = lambda i: (i, 0)
    col = lambda i: (0, i)
    const = lambda i: (0, 0)
    gate = lambda n: pl.BlockSpec((tm, D_MODEL), lambda i: (i, C_GATE // D_MODEL + n))
    return pl.pallas_call(
        functools.partial(_merge_kernel, alpha=alpha),
        grid=(t // tm,),
        in_specs=[pl.BlockSpec((tm, 512), row), pl.BlockSpec((512, tm), col), pl.BlockSpec((512, tm), col),
                  gate(0), gate(1), gate(2), pl.BlockSpec((tm, D_MODEL), row)]
                 + [pl.BlockSpec((512, D_MODEL), const)] * 3 + [pl.BlockSpec((D_MODEL, D_MODEL), const)]
                 + [pl.BlockSpec((1, D_MODEL), const)] * 2,
        out_specs=[pl.BlockSpec((tm, D_MODEL), row)] * 2,
        out_shape=[jax.ShapeDtypeStruct((t, D_MODEL), F32), jax.ShapeDtypeStruct((t, D_MODEL), BF16)],
        compiler_params=_params("parallel"),
        name="merge_ln1",
    )(ya, ybt, yct, h, h, h, x, wa, wb, wc, wo, lg, lb)


def _ffn_body(tv_ref, x_ref, w1_ref, w3_ref, w2_ref, xb, acc):
    i, j = pl.program_id(0), pl.program_id(1)

    @pl.when(j == 0)
    def _():
        acc[...] = jnp.zeros_like(acc)
        xb[...] = x_ref[...].astype(BF16)

    @pl.when(tv_ref[i] != 0)
    def _():
        x = xb[...]
        a = _dot(x, w1_ref[0])
        b = _dot(x, w3_ref[0])
        mid = (a * jax.nn.sigmoid(a)) * b
        acc[...] += _dot(mid.astype(BF16), w2_ref[0])


def _ffn_dense_kernel(te_ref, tv_ref, x_ref, w1_ref, w3_ref, w2_ref, r_ref, lg_ref, lb_ref, o32_ref, o16_ref,
                      xb, acc, *, alpha):
    _ffn_body(tv_ref, x_ref, w1_ref, w3_ref, w2_ref, xb, acc)

    @pl.when(pl.program_id(1) == pl.num_programs(1) - 1)
    def _():
        y = _layernorm(alpha * r_ref[...] + acc[...], lg_ref[...], lb_ref[...])
        o32_ref[...] = y
        o16_ref[...] = y.astype(BF16)


def _ffn_group_kernel(te_ref, tv_ref, x_ref, w1_ref, w3_ref, w2_ref, o_ref, xb, acc):
    _ffn_body(tv_ref, x_ref, w1_ref, w3_ref, w2_ref, xb, acc)

    @pl.when(pl.program_id(1) == pl.num_programs(1) - 1)
    def _():
        o_ref[...] = acc[...]


def _ffn_specs(tm, tf, d, nf):
    fidx = lambda j, v: j * v + (nf - 1) * (1 - v)
    return [pl.BlockSpec((tm, d), lambda i, j, te, tv: (i, 0)),
            pl.BlockSpec((1, d, tf), lambda i, j, te, tv: (te[i], 0, fidx(j, tv[i]))),
            pl.BlockSpec((1, d, tf), lambda i, j, te, tv: (te[i], 0, fidx(j, tv[i]))),
            pl.BlockSpec((1, tf, d), lambda i, j, te, tv: (te[i], fidx(j, tv[i]), 0))]


def _ffn_dense(x16, x32, w1, w3, w2, lg, lb, *, alpha, tm, tf):
    t, d = x16.shape
    nt, nf = t // tm, w1.shape[2] // tf
    row = lambda i, j, te, tv: (i, 0)
    const = lambda i, j, te, tv: (0, 0)
    te = jnp.zeros((nt,), jnp.int32)
    tv = jnp.ones((nt,), jnp.int32)
    return pl.pallas_call(
        functools.partial(_ffn_dense_kernel, alpha=alpha),
        grid_spec=pltpu.PrefetchScalarGridSpec(
            num_scalar_prefetch=2, grid=(nt, nf),
            in_specs=_ffn_specs(tm, tf, d, nf) + [pl.BlockSpec((tm, d), row), pl.BlockSpec((1, d), const),
                                                   pl.BlockSpec((1, d), const)],
            out_specs=[pl.BlockSpec((tm, d), row)] * 2,
            scratch_shapes=[pltpu.VMEM((tm, d), BF16), pltpu.VMEM((tm, d), F32)]),
        out_shape=[jax.ShapeDtypeStruct((t, d), F32), jax.ShapeDtypeStruct((t, d), BF16)],
        compiler_params=_params("parallel", "arbitrary"),
        name="ffn_dense",
    )(te, tv, x16, w1, w3, w2, x32, lg, lb)


def _ffn_grouped(xs, te, tv, w1, w3, w2, *, tm, tf):
    n, d = xs.shape
    nt, nf = n // tm, w1.shape[2] // tf
    row = lambda i, j, te, tv: (i, 0)
    return pl.pallas_call(
        _ffn_group_kernel,
        grid_spec=pltpu.PrefetchScalarGridSpec(
            num_scalar_prefetch=2, grid=(nt, nf),
            in_specs=_ffn_specs(tm, tf, d, nf),
            out_specs=pl.BlockSpec((tm, d), row),
            scratch_shapes=[pltpu.VMEM((tm, d), BF16), pltpu.VMEM((tm, d), F32)]),
        out_shape=jax.ShapeDtypeStruct((n, d), F32),
        compiler_params=_params("parallel", "arbitrary"),
        name="ffn_grouped",
    )(te, tv, xs, w1, w3, w2)


def _router_kernel(x_ref, wh_ref, wl_ref, idx_ref, wt_ref):
    x = x_ref[...]
    xh = x.astype(BF16)
    xl = (x - xh.astype(F32)).astype(BF16)
    logits = _dot(xh, wh_ref[...]) + _dot(xl, wh_ref[...]) + _dot(xh, wl_ref[...])
    lane = lax.broadcasted_iota(jnp.int32, logits.shape, 1)
    ninf = jnp.float32(-jnp.inf)
    l1 = jnp.where(lane < N_EXPERTS, logits, ninf)
    m1 = jnp.max(l1, axis=-1, keepdims=True)
    i1 = jnp.min(jnp.where(l1 == m1, lane, LANES), axis=-1, keepdims=True)
    l2 = jnp.where(lane == i1, ninf, l1)
    m2 = jnp.max(l2, axis=-1, keepdims=True)
    i2 = jnp.min(jnp.where(l2 == m2, lane, LANES), axis=-1, keepdims=True)
    e = jnp.exp(m2 - m1)
    den = 1.0 + e
    idx_ref[...] = jnp.where(lane == 0, i1, jnp.where(lane == 1, i2, 0))
    wt_ref[...] = jnp.where(lane == 0, 1.0 / den, jnp.where(lane == 1, e / den, 0.0))


def _router(x32, wh, wl, *, tm):
    t, d = x32.shape
    row = lambda i: (i, 0)
    const = lambda i: (0, 0)
    return pl.pallas_call(
        _router_kernel,
        grid=(t // tm,),
        in_specs=[pl.BlockSpec((tm, d), row), pl.BlockSpec((d, LANES), const), pl.BlockSpec((d, LANES), const)],
        out_specs=[pl.BlockSpec((tm, LANES), row)] * 2,
        out_shape=[jax.ShapeDtypeStruct((t, LANES), jnp.int32), jax.ShapeDtypeStruct((t, LANES), F32)],
        compiler_params=_params("parallel"),
        name="router",
    )(x32, wh, wl)


def _row_copy(src_hbm, src_row, dst_ref, dst_row, sem):
    return pltpu.make_async_copy(src_hbm.at[pl.ds(src_row, 1)], dst_ref.at[pl.ds(dst_row, 1)], sem)


def _for_rows(n, fn):
    def group(gi, c):
        for u in range(DMA_UNROLL):
            fn(gi * DMA_UNROLL + u, u % 2)
        return c

    lax.fori_loop(0, n // DMA_UNROLL, group, 0)


def _gather_kernel(src_ref, x_hbm, o_ref, sem):
    tm = o_ref.shape[0]
    base = pl.program_id(0) * tm
    _for_rows(tm, lambda r, par: _row_copy(x_hbm, src_ref[base + r], o_ref, r, sem).start(priority=par))
    _for_rows(tm, lambda r, par: _row_copy(x_hbm, 0, o_ref, r, sem).wait())


def _gather_rows(x32, src, *, tm):
    n = src.shape[0]
    d = x32.shape[1]
    return pl.pallas_call(
        _gather_kernel,
        grid_spec=pltpu.PrefetchScalarGridSpec(
            num_scalar_prefetch=1, grid=(n // tm,),
            in_specs=[pl.BlockSpec(memory_space=pl.ANY)],
            out_specs=pl.BlockSpec((tm, d), lambda i, src: (i, 0)),
            scratch_shapes=[pltpu.SemaphoreType.DMA(())]),
        out_shape=jax.ShapeDtypeStruct((n, d), x32.dtype),
        compiler_params=_params("arbitrary"),
        name="moe_gather",
    )(src, x32)


def _combine_kernel(pos_ref, x_ref, wt_ref, ys_hbm, lg_ref, lb_ref, o32_ref, o16_ref, buf0, buf1, sem, *, alpha):
    tm = x_ref.shape[0]
    base = pl.program_id(0) * tm

    def issue(r, par):
        _row_copy(ys_hbm, pos_ref[2 * (base + r)], buf0, r, sem).start(priority=0)
        _row_copy(ys_hbm, pos_ref[2 * (base + r) + 1], buf1, r, sem).start(priority=1)

    def wait(r, par):
        _row_copy(ys_hbm, 0, buf0, r, sem).wait()
        _row_copy(ys_hbm, 0, buf1, r, sem).wait()

    _for_rows(tm, issue)
    _for_rows(tm, wait)
    wt = wt_ref[...]
    ff = wt[:, 0:1] * buf0[...] + wt[:, 1:2] * buf1[...]
    y = _layernorm(alpha * x_ref[...] + ff, lg_ref[...], lb_ref[...])
    o32_ref[...] = y
    o16_ref[...] = y.astype(BF16)


def _combine(pos, x32, wts, ys, lg, lb, *, alpha, tm):
    t, d = x32.shape
    row = lambda i, pos: (i, 0)
    const = lambda i, pos: (0, 0)
    return pl.pallas_call(
        functools.partial(_combine_kernel, alpha=alpha),
        grid_spec=pltpu.PrefetchScalarGridSpec(
            num_scalar_prefetch=1, grid=(t // tm,),
            in_specs=[pl.BlockSpec((tm, d), row), pl.BlockSpec((tm, LANES), row), pl.BlockSpec(memory_space=pl.ANY),
                      pl.BlockSpec((1, d), const), pl.BlockSpec((1, d), const)],
            out_specs=[pl.BlockSpec((tm, d), row)] * 2,
            scratch_shapes=[pltpu.VMEM((tm, d), F32), pltpu.VMEM((tm, d), F32), pltpu.SemaphoreType.DMA(())]),
        out_shape=[jax.ShapeDtypeStruct((t, d), F32), jax.ShapeDtypeStruct((t, d), BF16)],
        compiler_params=_params("arbitrary"),
        name="moe_combine",
    )(pos, x32, wts, ys, lg, lb)


def _route_meta(idx, tm):
    t = idx.shape[0]
    a = 2 * t
    e = idx.reshape(a)
    onehot = (e[:, None] == jnp.arange(N_EXPERTS, dtype=jnp.int32)[None, :]).astype(jnp.int32)
    csum = jnp.cumsum(onehot, axis=0)
    rank = jnp.sum((csum - onehot) * onehot, axis=1)
    cnt = csum[-1]
    pcnt = ((cnt + tm - 1) // tm) * tm
    pend = jnp.cumsum(pcnt)
    pos = jnp.sum(onehot * (pend - pcnt)[None, :], axis=1) + rank
    n_rows = a + N_EXPERTS * tm
    src = jnp.zeros((n_rows,), jnp.int32).at[pos].set(jnp.arange(a, dtype=jnp.int32) // 2)
    start = jnp.arange(n_rows // tm, dtype=jnp.int32) * tm
    te = jnp.minimum(jnp.sum((start[:, None] >= pend[None, :]).astype(jnp.int32), axis=1), N_EXPERTS - 1)
    tv = (start < pend[-1]).astype(jnp.int32)
    return pos.astype(jnp.int32), src, te.astype(jnp.int32), tv


def _prep_w_in(w):
    d = w.shape[0]
    z = lambda n: jnp.zeros((d, n), w.dtype)
    na_q, na_k, na_v = w[:, 0:512] * (HEAD_DIM ** -0.5), w[:, 512:1024], w[:, 1024:1536]
    g_q, g_k, g_v = w[:, 1536:2048], w[:, 2048:2176], w[:, 2176:2304]
    c_q, c_kv, k_r, gate = w[:, 2304:2688], w[:, 2688:2944], w[:, 2944:2976], w[:, 2976:]
    kr_blk = jnp.concatenate([z(MLA_NOPE_DIM), k_r, z(LANES - MLA_NOPE_DIM - MLA_ROPE_DIM)], axis=1)
    out = jnp.concatenate([na_q, na_k, na_v, g_q, g_k, g_v, c_kv, kr_blk, c_q, gate], axis=1).astype(BF16)
    assert out.shape[1] == H_COLS
    return out


def _prep_mla_w(w_uq, w_ukv):
    heads = w_uq.shape[1] // (MLA_NOPE_DIM + MLA_ROPE_DIM)
    wq = w_uq.reshape(MLA_Q_LORA, heads, MLA_NOPE_DIM + MLA_ROPE_DIM)
    wq = jnp.pad(wq, ((0, 0), (0, 0), (0, LANES - MLA_NOPE_DIM - MLA_ROPE_DIM))).reshape(MLA_Q_LORA, heads * LANES)
    wkv = w_ukv.reshape(MLA_KV_LORA, heads, LANES)
    wk = jnp.pad(wkv[:, :, :MLA_NOPE_DIM], ((0, 0), (0, 0), (0, LANES - MLA_NOPE_DIM))).reshape(MLA_KV_LORA, heads * LANES)
    wvt = wkv[:, :, MLA_NOPE_DIM:].reshape(MLA_KV_LORA, heads * HEAD_DIM).T
    return wq.astype(BF16), wk.astype(BF16), wvt.astype(BF16)


def _rope_tables(seq):
    pos = jnp.arange(seq, dtype=jnp.int32)
    half = 16
    inv_freq = ROPE_THETA ** (-jnp.arange(half, dtype=F32) / half)

    def cs(p):
        ang = p.astype(F32)[:, None] * inv_freq[None, :]
        return jnp.cos(ang), jnp.sin(ang)

    cr, sr = cs(pos // GRID_W)
    cc, sc = cs(pos % GRID_W)
    cp, sp = cs(pos)
    one = lambda n: jnp.ones((seq, n), F32)
    zero = lambda n: jnp.zeros((seq, n), F32)
    g_cos = jnp.tile(jnp.concatenate([cr, cr, cc, cc], axis=1), (1, 2))
    g_sin = jnp.tile(jnp.concatenate([-sr, sr, -sc, sc], axis=1), (1, 2))
    m_cos = jnp.concatenate([one(MLA_NOPE_DIM), cp, cp, one(32)], axis=1)
    m_sin = jnp.concatenate([zero(MLA_NOPE_DIM), -sp, sp, zero(32)], axis=1)
    return g_cos, g_sin, m_cos, m_sin


def kernel(x_prompt, x_sample, w_in, na_rpb, gqa_q_norm, gqa_k_norm, mla_q_norm, mla_w_uq, mla_kv_norm, mla_w_ukv, w_branch_a, w_branch_b, w_branch_c, w_out, ln1_g, ln1_b, ln2_g, ln2_b, ffn_w1, ffn_w3, ffn_w2, moe_router, moe_w1, moe_w3, moe_w2):
    depth = w_in.shape[0]
    alpha = float((2 * depth) ** 0.25)
    groups = [(x_prompt.shape[0], x_prompt.shape[1]), (x_sample.shape[0], x_sample.shape[1])]
    d = x_prompt.shape[2]
    assert d == D_MODEL
    x32 = jnp.concatenate([x_prompt.reshape(-1, d), x_sample.reshape(-1, d)], axis=0)
    t = x32.shape[0]
    x16 = x32.astype(BF16)

    tabs = [_rope_tables(s) for _, s in groups]
    g_cos, g_sin, m_cos, m_sin = [jnp.concatenate([jnp.tile(tabs[g][n], (groups[g][0], 1)) for g in range(2)], axis=0)
                                  for n in range(4)]
    blk = np.kron(np.eye(2), np.ones((HEAD_DIM, HEAD_DIM)))
    ones_bd = jnp.asarray(blk, BF16)

    tm_proj = _pick(t, 2048)
    tm_tok = _pick(t, 512, LANES)
    tm_moe = _pick(t, 512)
    row2 = lambda v: v.reshape(1, -1).astype(F32)

    for i in range(depth):
        h = _matmul(x16, _prep_w_in(w_in[i]), tm=tm_proj, tn=512)
        qg = row2(jnp.tile(gqa_q_norm[i] * (HEAD_DIM ** -0.5 * LOG2E), 2))
        kg = row2(jnp.tile(gqa_k_norm[i], 2))
        gq, gkd, gvt = _gqa_prep(h, qg, kg, g_cos, g_sin, ones_bd, tm=tm_tok)
        wq, wk, wvt = _prep_mla_w(mla_w_uq[i], mla_w_ukv[i])
        mq, mk, mvt = _mla_prep(h, row2(mla_q_norm[i]), row2(mla_kv_norm[i]), wq, wk, wvt, m_cos, m_sin, tm=tm_tok)
        bias = _na_bias_planes(na_rpb[i])
        ya, ybt, yct = [], [], []
        off = 0
        for batch, seq in groups:
            tq = _pick(seq, 256, LANES)
            ya.append(_na_attn(h, bias, tok_off=off, batch=batch, seq=seq))
            n_sub = seq // SUB_KEYS
            ybt.append(_gqa_attn(gq, gkd, gvt, tok_off=off, batch=batch, seq=seq, tq=tq,
                                 groups_per_step=2 if GQA_GROUP * n_sub < TILES_PER_BODY else 1))
            yct.append(_mla_attn(mq, mk, mvt, tok_off=off, batch=batch, seq=seq, tq=tq,
                                 heads_per_step=min(8, max(2, TILES_PER_BODY // n_sub))))
            off += batch * seq
        ya = jnp.concatenate(ya, axis=0)
        ybt = jnp.concatenate(ybt, axis=1)
        yct = jnp.concatenate(yct, axis=1)
        x32, x16 = _merge(ya, ybt, yct, h, x32, w_branch_a[i].astype(BF16), w_branch_b[i].astype(BF16),
                          w_branch_c[i].astype(BF16), w_out[i].astype(BF16), row2(ln1_g[i]), row2(ln1_b[i]),
                          alpha=alpha, tm=tm_tok)
        j = i // 2
        if i % 2 == 0:
            f = ffn_w1.shape[2]
            x32, x16 = _ffn_dense(x16, x32, ffn_w1[j:j + 1].astype(BF16), ffn_w3[j:j + 1].astype(BF16),
                                  ffn_w2[j:j + 1].astype(BF16), row2(ln2_g[i]), row2(ln2_b[i]),
                                  alpha=alpha, tm=_pick(t, 256, LANES), tf=_pick(f, 2816, 2 * LANES))
        else:
            rw = jnp.pad(moe_router[j], ((0, 0), (0, LANES - N_EXPERTS)))
            rw_hi = rw.astype(BF16)
            rw_lo = (rw - rw_hi.astype(F32)).astype(BF16)
            idx, wts = _router(x32, rw_hi, rw_lo, tm=tm_tok)
            pos, src, te, tv = _route_meta(idx[:, :2], tm_moe)
            xs = _gather_rows(x32, src, tm=tm_moe)
            f = moe_w1.shape[3]
            ys = _ffn_grouped(xs, te, tv, moe_w1[j].astype(BF16), moe_w3[j].astype(BF16),
                              moe_w2[j].astype(BF16), tm=tm_moe, tf=_pick(f, 1792, 2 * LANES))
            x32, x16 = _combine(pos, x32, wts, ys, row2(ln2_g[i]), row2(ln2_b[i]), alpha=alpha, tm=_pick(t, 256))

    tp = groups[0][0] * groups[0][1]
    return (x32[:tp].reshape(x_prompt.shape), x32[tp:].reshape(x_sample.shape))
```

```python
import functools
import math

import numpy as np
import jax
import jax.numpy as jnp
from jax import lax
from jax.experimental import pallas as pl
from jax.experimental.pallas import tpu as pltpu

F32 = jnp.float32
BF16 = jnp.bfloat16

D_MODEL = 1024
GRID_W = 64
HEAD_DIM = 64
NA_WIN_H = 8
NA_WIN_W = 16
MLA_Q_LORA = 384
MLA_KV_LORA = 256
MLA_NOPE_DIM = 64
MLA_ROPE_DIM = 32
N_EXPERTS = 8
GQA_GROUP = 4
ROPE_THETA = 10000.0
RMS_EPS = 1e-6
LN_EPS = 1e-5
NEG_INF = -1e30
LOG2E = math.log2(math.e)

LANES = 128
BF16_ROWS = 16
ROWS_PER_STEP = 8
DMA_UNROLL = 8
SUB_KEYS = 256
QK_AHEAD = 6
TILES_PER_BODY = 64
NA_AHEAD = 2

C_NAQ, C_NAK, C_NAV = 0, 512, 1024
C_GQ, C_GK, C_GV = 1536, 2048, 2176
C_CKV, C_KR, C_CQ, C_GATE = 2304, 2560, 2688, 3072
H_COLS = 6144

VMEM_LIMIT = 56 * 1024 * 1024


def _params(*sem):
    return pltpu.CompilerParams(dimension_semantics=sem, vmem_limit_bytes=VMEM_LIMIT)


def _pick(n, pref, mult=8):
    t = min(pref, n)
    while t > mult and (n % t or t % mult):
        t -= mult
    assert n % t == 0, (n, pref)
    return t


def _dot(a, b):
    return jnp.dot(a, b, preferred_element_type=F32)


def _dot_nt(a, b):
    return lax.dot_general(a, b, (((1,), (1,)), ((), ())), preferred_element_type=F32)


def _dot_tn(a, b):
    return lax.dot_general(a, b, (((0,), (0,)), ((), ())), preferred_element_type=F32)


def _layernorm(z, g, b):
    mu = jnp.mean(z, axis=-1, keepdims=True)
    zc = z - mu
    var = jnp.mean(zc * zc, axis=-1, keepdims=True)
    return zc * lax.rsqrt(var + LN_EPS) * g + b


def _mm_kernel(x_ref, w_ref, o_ref):
    o_ref[...] = _dot(x_ref[...], w_ref[...]).astype(o_ref.dtype)


def _matmul(x, w, *, tm, tn, out_dtype=BF16):
    m, k = x.shape
    n = w.shape[1]
    return pl.pallas_call(
        _mm_kernel,
        grid=(m // tm, n // tn),
        in_specs=[pl.BlockSpec((tm, k), lambda i, j: (i, 0)),
                  pl.BlockSpec((k, tn), lambda i, j: (0, j))],
        out_specs=pl.BlockSpec((tm, tn), lambda i, j: (i, j)),
        out_shape=jax.ShapeDtypeStruct((m, n), out_dtype),
        compiler_params=_params("parallel", "parallel"),
        name="proj_in",
    )(x, w)


def _rope_apply(y, cos, sin_signed, lane):
    partner = jnp.where(lane % 32 < 16, pltpu.roll(y, LANES - 16, 1), pltpu.roll(y, 16, 1))
    return y * cos + partner * sin_signed


def _group_sumsq(x, ones_bd):
    s = x * x
    s_hi = s.astype(BF16)
    s_lo = (s - s_hi.astype(F32)).astype(BF16)
    return _dot(s_hi, ones_bd) + _dot(s_lo, ones_bd)


def _gqa_prep_kernel(q_ref, k_ref, v_ref, qg_ref, kg_ref, c_ref, s_ref, ones_ref, qo_ref, kd_ref, vt_ref):
    tm = q_ref.shape[0]
    lane = lax.broadcasted_iota(jnp.int32, (tm, LANES), 1)
    cos, sin_signed, ones_bd = c_ref[...], s_ref[...], ones_ref[...]

    def norm_rope(x, gain):
        y = x * lax.rsqrt(_group_sumsq(x, ones_bd) * (1.0 / HEAD_DIM) + RMS_EPS) * gain
        return _rope_apply(y, cos, sin_signed, lane)

    for j in range(q_ref.shape[1] // LANES):
        sl = slice(j * LANES, (j + 1) * LANES)
        qo_ref[sl, :] = norm_rope(q_ref[:, sl].astype(F32), qg_ref[...]).T.astype(BF16)
    low = lane < HEAD_DIM
    k = norm_rope(k_ref[...].astype(F32), kg_ref[...])
    k_sw = pltpu.roll(k, HEAD_DIM, 1)
    kd_ref[:, 0:LANES] = jnp.where(low, k, k_sw).astype(BF16)
    kd_ref[:, LANES:2 * LANES] = jnp.where(low, k_sw, k).astype(BF16)
    vt_ref[...] = v_ref[...].astype(F32).T.astype(BF16)


def _gqa_prep(h, qg, kg, cos, sin_signed, ones_bd, *, tm):
    t = h.shape[0]
    row = lambda i: (i, 0)
    const = lambda i: (0, 0)
    return pl.pallas_call(
        _gqa_prep_kernel,
        grid=(t // tm,),
        in_specs=[pl.BlockSpec((tm, 512), lambda i: (i, C_GQ // 512)),
                  pl.BlockSpec((tm, LANES), lambda i: (i, C_GK // LANES)),
                  pl.BlockSpec((tm, LANES), lambda i: (i, C_GV // LANES)),
                  pl.BlockSpec((1, LANES), const), pl.BlockSpec((1, LANES), const),
                  pl.BlockSpec((tm, LANES), row), pl.BlockSpec((tm, LANES), row),
                  pl.BlockSpec((LANES, LANES), const)],
        out_specs=[pl.BlockSpec((512, tm), lambda i: (0, i)), pl.BlockSpec((tm, 256), row),
                   pl.BlockSpec((LANES, tm), lambda i: (0, i))],
        out_shape=[jax.ShapeDtypeStruct((512, t), BF16), jax.ShapeDtypeStruct((t, 256), BF16),
                   jax.ShapeDtypeStruct((LANES, t), BF16)],
        compiler_params=_params("parallel"),
        name="gqa_prep",
    )(h, h, h, qg, kg, cos, sin_signed, ones_bd)


def _mla_prep_kernel(cq_ref, ckv_ref, kr_ref, qn_ref, kvn_ref, wq_ref, wk_ref, wvt_ref, c_ref, s_ref,
                     q_ref, k_ref, vt_ref, *, scale):
    tm = cq_ref.shape[0]
    lane = lax.broadcasted_iota(jnp.int32, (tm, LANES), 1)
    cos, sin_signed = c_ref[...], s_ref[...]

    def rms(x, g):
        return x * lax.rsqrt(jnp.mean(x * x, axis=-1, keepdims=True) + RMS_EPS) * g

    cq = rms(cq_ref[...].astype(F32), qn_ref[...]).astype(BF16)
    ckv = rms(ckv_ref[...].astype(F32), kvn_ref[...]).astype(BF16)
    q = _dot(cq, wq_ref[...])
    k = _dot(ckv, wk_ref[...])
    vt_ref[...] = _dot_nt(wvt_ref[...], ckv).astype(BF16)
    k_rope = _rope_apply(kr_ref[...].astype(F32), cos, sin_signed, lane)
    for hd in range(q.shape[1] // LANES):
        sl = slice(hd * LANES, (hd + 1) * LANES)
        q_ref[sl, :] = (_rope_apply(q[:, sl], cos, sin_signed, lane) * scale).T.astype(BF16)
        k_ref[:, sl] = (k[:, sl] + k_rope).astype(BF16)


def _mla_prep(h, qn, kvn, wq, wk, wvt, cos, sin_signed, *, tm):
    t = h.shape[0]
    row = lambda i: (i, 0)
    const = lambda i: (0, 0)
    scale = float((MLA_NOPE_DIM + MLA_ROPE_DIM) ** -0.5 * LOG2E)
    return pl.pallas_call(
        functools.partial(_mla_prep_kernel, scale=scale),
        grid=(t // tm,),
        in_specs=[pl.BlockSpec((tm, MLA_Q_LORA), lambda i: (i, C_CQ // MLA_Q_LORA)),
                  pl.BlockSpec((tm, MLA_KV_LORA), lambda i: (i, C_CKV // MLA_KV_LORA)),
                  pl.BlockSpec((tm, LANES), lambda i: (i, C_KR // LANES)),
                  pl.BlockSpec((1, MLA_Q_LORA), const), pl.BlockSpec((1, MLA_KV_LORA), const),
                  pl.BlockSpec(wq.shape, const), pl.BlockSpec(wk.shape, const), pl.BlockSpec(wvt.shape, const),
                  pl.BlockSpec((tm, LANES), row), pl.BlockSpec((tm, LANES), row)],
        out_specs=[pl.BlockSpec((1024, tm), lambda i: (0, i)), pl.BlockSpec((tm, 1024), row),
                   pl.BlockSpec((512, tm), lambda i: (0, i))],
        out_shape=[jax.ShapeDtypeStruct((1024, t), BF16), jax.ShapeDtypeStruct((t, 1024), BF16),
                   jax.ShapeDtypeStruct((512, t), BF16)],
        compiler_params=_params("parallel"),
        name="mla_prep",
    )(h, h, h, qn, kvn, wq, wk, wvt, cos, sin_signed)


def _flash_t(q_ops, kv_fn, n_chunks, n_sub, tq):
    init = tuple((jnp.full((1, tq), NEG_INF, F32), jnp.zeros((HEAD_DIM + BF16_ROWS, tq), F32)) for _ in q_ops)

    def body(c, carry):
        carry = list(carry)
        tiles = [(s, hd) for s in range(n_sub) for hd in range(len(q_ops))]
        kvs = {}

        def scores(s, hd):
            if s not in kvs:
                kvs[s] = kv_fn(c, s)
            return _dot(kvs[s][hd][0], q_ops[hd])

        pending = [scores(*tl) for tl in tiles[:QK_AHEAD]]
        for n, (s, hd) in enumerate(tiles):
            if n + QK_AHEAD < len(tiles):
                pending.append(scores(*tiles[n + QK_AHEAD]))
            st = pending.pop(0)
            m, acc = carry[hd]
            m_new = jnp.maximum(m, jnp.max(st, axis=0, keepdims=True))
            alpha = jnp.exp2(m - m_new)
            pt = jnp.exp2(st - m_new).astype(BF16)
            carry[hd] = (m_new, acc * alpha + _dot(kvs[s][hd][1], pt))
        return tuple(carry)

    res = lax.fori_loop(0, n_chunks, body, init)
    return [acc[0:HEAD_DIM] / acc[HEAD_DIM:HEAD_DIM + 1] for _, acc in res]


def _gqa_attn_kernel(q_ref, kd_ref, vt_ref, o_ref, *, tk):
    tq = q_ref.shape[1]
    n_heads = q_ref.shape[0] // HEAD_DIM
    zero = jnp.zeros((HEAD_DIM, tq), BF16)
    q_ops = [jnp.concatenate([q_ref[hd * HEAD_DIM:(hd + 1) * HEAD_DIM, :], zero], axis=0) for hd in range(n_heads)]
    ones = jnp.ones((BF16_ROWS, SUB_KEYS), BF16)

    def kv_fn(c, s):
        rows = pl.ds(pl.multiple_of(c * tk + s * SUB_KEYS, SUB_KEYS), SUB_KEYS)
        per_group = [(kd_ref[rows, g * LANES:(g + 1) * LANES],
                      jnp.concatenate([vt_ref[g * HEAD_DIM:(g + 1) * HEAD_DIM, rows], ones], axis=0))
                     for g in range(n_heads // GQA_GROUP)]
        return [per_group[hd // GQA_GROUP] for hd in range(n_heads)]

    outs = _flash_t(q_ops, kv_fn, kd_ref.shape[0] // tk, tk // SUB_KEYS, tq)
    for hd, o in enumerate(outs):
        o_ref[hd * HEAD_DIM:(hd + 1) * HEAD_DIM, :] = o.astype(o_ref.dtype)


def _gqa_attn(q, kd, vt, *, tok_off, batch, seq, tq, groups_per_step):
    n_groups = vt.shape[0] // HEAD_DIM
    gps = groups_per_step
    tk = _pick(seq, TILES_PER_BODY // (gps * GQA_GROUP) * SUB_KEYS, SUB_KEYS)
    assert tok_off % seq == 0 and seq % tq == 0 and n_groups % gps == 0
    nq = seq // tq
    return pl.pallas_call(
        functools.partial(_gqa_attn_kernel, tk=tk),
        grid=(batch, n_groups // gps, nq),
        in_specs=[pl.BlockSpec((gps * GQA_GROUP * HEAD_DIM, tq), lambda b, g, i: (g, tok_off // tq + b * nq + i)),
                  pl.BlockSpec((seq, gps * LANES), lambda b, g, i: (tok_off // seq + b, g)),
                  pl.BlockSpec((gps * HEAD_DIM, seq), lambda b, g, i: (g, tok_off // seq + b))],
        out_specs=pl.BlockSpec((gps * GQA_GROUP * HEAD_DIM, tq), lambda b, g, i: (g, b * nq + i)),
        out_shape=jax.ShapeDtypeStruct((512, batch * seq), BF16),
        compiler_params=_params("parallel", "parallel", "parallel"),
        name="gqa_attn",
    )(q, kd, vt)


def _mla_attn_kernel(q_ref, k_ref, vt_ref, o_ref, *, tk):
    tq = q_ref.shape[1]
    n_heads = q_ref.shape[0] // LANES
    q_ops = [q_ref[hd * LANES:(hd + 1) * LANES, :] for hd in range(n_heads)]
    ones = jnp.ones((BF16_ROWS, SUB_KEYS), BF16)

    def kv_fn(c, s):
        rows = pl.ds(pl.multiple_of(c * tk + s * SUB_KEYS, SUB_KEYS), SUB_KEYS)
        return [(k_ref[rows, hd * LANES:(hd + 1) * LANES],
                 jnp.concatenate([vt_ref[hd * HEAD_DIM:(hd + 1) * HEAD_DIM, rows], ones], axis=0))
                for hd in range(n_heads)]

    outs = _flash_t(q_ops, kv_fn, k_ref.shape[0] // tk, tk // SUB_KEYS, tq)
    for hd, o in enumerate(outs):
        o_ref[hd * HEAD_DIM:(hd + 1) * HEAD_DIM, :] = o.astype(o_ref.dtype)


def _mla_attn(q, k, vt, *, tok_off, batch, seq, tq, heads_per_step):
    n_heads = vt.shape[0] // HEAD_DIM
    hps = heads_per_step
    tk = _pick(seq, TILES_PER_BODY // hps * SUB_KEYS, SUB_KEYS)
    assert tok_off % seq == 0 and seq % tq == 0 and n_heads % hps == 0
    nq = seq // tq
    return pl.pallas_call(
        functools.partial(_mla_attn_kernel, tk=tk),
        grid=(batch, n_heads // hps, nq),
        in_specs=[pl.BlockSpec((hps * LANES, tq), lambda b, p, i: (p, tok_off // tq + b * nq + i)),
                  pl.BlockSpec((seq, hps * LANES), lambda b, p, i: (tok_off // seq + b, p)),
                  pl.BlockSpec((hps * HEAD_DIM, seq), lambda b, p, i: (p, tok_off // seq + b))],
        out_specs=pl.BlockSpec((hps * HEAD_DIM, tq), lambda b, p, i: (p, b * nq + i)),
        out_shape=jax.ShapeDtypeStruct((512, batch * seq), BF16),
        compiler_params=_params("parallel", "parallel", "parallel"),
        name="mla_attn",
    )(q, k, vt)


NA_PAIR_WIN = (NA_WIN_H + 2) * GRID_W
NA_REL_ROWS = 2 * NA_WIN_H - 1


def _na_bias_planes(rpb):
    c = np.arange(GRID_W)
    kc = np.arange(GRID_W)
    c0 = np.clip(c - NA_WIN_W // 2, 0, GRID_W - NA_WIN_W)
    ok = (kc[None, :] >= c0[:, None]) & (kc[None, :] < c0[:, None] + NA_WIN_W)
    pad = GRID_W - NA_WIN_W
    padded = jnp.pad(rpb.astype(F32), ((0, 0), (0, 0), (pad, pad)))
    planes = jnp.stack([padded[:, :, GRID_W - 1 - ci:2 * GRID_W - 1 - ci] for ci in range(GRID_W)], axis=2)
    planes = jnp.swapaxes(jnp.where(ok[None, None], planes, NEG_INF), 2, 3)
    n_pairs = rpb.shape[0] // 2
    planes = planes.reshape(n_pairs, 2, NA_REL_ROWS, GRID_W, GRID_W)
    planes = jnp.concatenate([planes[:, 0], planes[:, 1]], axis=-1)
    return jnp.pad(planes, ((0, 0), (0, 1), (0, 0), (0, 0)), constant_values=NEG_INF)


def _na_kernel(q_ref, kp_ref, kc_ref, kn_ref, vp_ref, vc_ref, vn_ref, b_ref, o_ref, kbuf, vtbuf, *, rows):
    blk = ROWS_PER_STEP * GRID_W
    j = pl.program_id(2)
    for n, (kr, vr) in enumerate(((kp_ref, vp_ref), (kc_ref, vc_ref), (kn_ref, vn_ref))):
        kbuf[n * blk:(n + 1) * blk, :] = kr[...]
        vtbuf[:, n * blk:(n + 1) * blk] = vr[...].astype(F32).T.astype(BF16)
    low = lax.broadcasted_iota(jnp.int32, (GRID_W, LANES), 1) < HEAD_DIM
    ones = jnp.ones((BF16_ROWS, NA_PAIR_WIN), BF16)
    n_tiles = ROWS_PER_STEP // 2

    def scores(ip):
        r = j * ROWS_PER_STEP + 2 * ip
        r0 = jnp.clip(r - NA_WIN_H // 2, 0, rows - NA_WIN_H)
        r0b = jnp.clip(r + 1 - NA_WIN_H // 2, 0, rows - NA_WIN_H)
        off = pl.multiple_of((r0 - j * ROWS_PER_STEP + ROWS_PER_STEP) * GRID_W, 2 * GRID_W)
        bias_rows = []
        for t in range(NA_PAIR_WIN // GRID_W):
            ia = r0 + t - r + (NA_WIN_H - 1) if t < NA_WIN_H else NA_REL_ROWS
            tb = r0 + t - r0b
            ib = jnp.where((tb >= 0) & (tb < NA_WIN_H), r0 + t - (r + 1) + (NA_WIN_H - 1), NA_REL_ROWS)
            bias_rows.append(jnp.concatenate([b_ref[0, ia], b_ref[0, ib]], axis=1))
        bias = jnp.concatenate(bias_rows, axis=0)
        q4 = []
        for i in (2 * ip, 2 * ip + 1):
            q = q_ref[i * GRID_W:(i + 1) * GRID_W, :]
            zero = jnp.zeros_like(q)
            q4 += [jnp.where(low, q, zero), jnp.where(low, zero, q)]
        st = _dot_nt(kbuf[pl.ds(off, NA_PAIR_WIN), :], jnp.concatenate(q4, axis=0)) + bias
        return off, st

    pending = [scores(ip) for ip in range(min(NA_AHEAD, n_tiles))]
    for ip in range(n_tiles):
        if ip + NA_AHEAD < n_tiles:
            pending.append(scores(ip + NA_AHEAD))
        off, st = pending.pop(0)
        pt = jnp.exp(st - jnp.max(st, axis=0, keepdims=True)).astype(BF16)
        v_aug = jnp.concatenate([vtbuf[:, pl.ds(off, NA_PAIR_WIN)], ones], axis=0)
        o = _dot(v_aug, pt)
        t = (o[0:LANES] / o[LANES:LANES + 1]).T
        for n in range(2):
            rows_out = slice((2 * ip + n) * GRID_W, (2 * ip + n + 1) * GRID_W)
            o_ref[rows_out, :] = jnp.where(low, t[2 * n * GRID_W:(2 * n + 1) * GRID_W],
                                           t[(2 * n + 1) * GRID_W:(2 * n + 2) * GRID_W]).astype(o_ref.dtype)


def _na_attn(h, bias, *, tok_off, batch, seq):
    rows = seq // GRID_W
    blk = ROWS_PER_STEP * GRID_W
    assert rows % ROWS_PER_STEP == 0 and rows >= NA_WIN_H and tok_off % blk == 0
    nb = rows // ROWS_PER_STEP
    n_pairs = bias.shape[0]
    base = tok_off // blk

    def tokmap(col0, shift):
        return lambda b, p, j: (base + b * nb + jnp.clip(j + shift, 0, nb - 1), col0 // LANES + p)

    kv_specs = [pl.BlockSpec((blk, LANES), tokmap(c0, s)) for c0 in (C_NAK, C_NAV) for s in (-1, 0, 1)]
    return pl.pallas_call(
        functools.partial(_na_kernel, rows=rows),
        grid=(batch, n_pairs, nb),
        in_specs=[pl.BlockSpec((blk, LANES), tokmap(C_NAQ, 0))] + kv_specs
                 + [pl.BlockSpec((1, NA_REL_ROWS + 1, GRID_W, LANES), lambda b, p, j: (p, 0, 0, 0))],
        out_specs=pl.BlockSpec((blk, LANES), lambda b, p, j: (b * nb + j, p)),
        out_shape=jax.ShapeDtypeStruct((batch * seq, 512), BF16),
        scratch_shapes=[pltpu.VMEM((3 * blk, LANES), BF16), pltpu.VMEM((LANES, 3 * blk), BF16)],
        compiler_params=_params("parallel", "parallel", "parallel"),
        name="na_attn",
    )(h, h, h, h, h, h, h, bias)


def _merge_kernel(ya0_ref, ya1_ref, ybt0_ref, ybt1_ref, yct0_ref, yct1_ref, g0_ref, g1_ref, g2_ref, x_ref,
                  wa_ref, wb_ref, wc_ref, wo_ref, lg_ref, lb_ref, o32_ref, o16_ref, *, alpha, tiles0):
    def gate(g_ref):
        return jax.nn.sigmoid(g_ref[...].astype(F32))

    in_group0 = pl.program_id(0) < tiles0
    pick = lambda r0, r1: jnp.where(in_group0, r0[...], r1[...])
    merged = (gate(g0_ref) * _dot(pick(ya0_ref, ya1_ref), wa_ref[...])
              + gate(g1_ref) * _dot_tn(pick(ybt0_ref, ybt1_ref), wb_ref[...])
              + gate(g2_ref) * _dot_tn(pick(yct0_ref, yct1_ref), wc_ref[...]))
    mix = _dot(merged.astype(BF16), wo_ref[...])
    y = _layernorm(alpha * x_ref[...] + mix, lg_ref[...], lb_ref[...])
    o32_ref[...] = y
    o16_ref[...] = y.astype(BF16)


def _merge(ya, ybt, yct, h, x, wa, wb, wc, wo, lg, lb, *, alpha, tm):
    t = x.shape[0]
    assert ya[0].shape[0] % tm == 0 and ya[0].shape[0] + ya[1].shape[0] == t
    tiles0 = ya[0].shape[0] // tm
    tiles1 = t // tm - tiles0
    row = lambda i: (i, 0)
    const = lambda i: (0, 0)
    idx0 = lambda i: jnp.minimum(i, tiles0 - 1)
    idx1 = lambda i: jnp.clip(i - tiles0, 0, tiles1 - 1)
    tok = lambda idx: pl.BlockSpec((tm, 512), lambda i: (idx(i), 0))
    feat = lambda idx: pl.BlockSpec((512, tm), lambda i: (0, idx(i)))
    gate = lambda n: pl.BlockSpec((tm, D_MODEL), lambda i: (i, C_GATE // D_MODEL + n))
    return pl.pallas_call(
        functools.partial(_merge_kernel, alpha=alpha, tiles0=tiles0),
        grid=(t // tm,),
        in_specs=[tok(idx0), tok(idx1), feat(idx0), feat(idx1), feat(idx0), feat(idx1),
                  gate(0), gate(1), gate(2), pl.BlockSpec((tm, D_MODEL), row)]
                 + [pl.BlockSpec((512, D_MODEL), const)] * 3 + [pl.BlockSpec((D_MODEL, D_MODEL), const)]
                 + [pl.BlockSpec((1, D_MODEL), const)] * 2,
        out_specs=[pl.BlockSpec((tm, D_MODEL), row)] * 2,
        out_shape=[jax.ShapeDtypeStruct((t, D_MODEL), F32), jax.ShapeDtypeStruct((t, D_MODEL), BF16)],
        compiler_params=_params("parallel"),
        name="merge_ln1",
    )(ya[0], ya[1], ybt[0], ybt[1], yct[0], yct[1], h, h, h, x, wa, wb, wc, wo, lg, lb)


def _ffn_body(tv_ref, x_ref, w1_ref, w3_ref, w2_ref, xb, acc):
    i, j = pl.program_id(0), pl.program_id(1)

    @pl.when(j == 0)
    def _():
        acc[...] = jnp.zeros_like(acc)
        xb[...] = x_ref[...].astype(BF16)

    @pl.when(tv_ref[i] != 0)
    def _():
        x = xb[...]
        a = _dot(x, w1_ref[0])
        b = _dot(x, w3_ref[0])
        mid = (a * jax.nn.sigmoid(a)) * b
        acc[...] += _dot(mid.astype(BF16), w2_ref[0])


def _ffn_dense_kernel(te_ref, tv_ref, x_ref, w1_ref, w3_ref, w2_ref, r_ref, lg_ref, lb_ref, o32_ref, o16_ref,
                      xb, acc, *, alpha):
    _ffn_body(tv_ref, x_ref, w1_ref, w3_ref, w2_ref, xb, acc)

    @pl.when(pl.program_id(1) == pl.num_programs(1) - 1)
    def _():
        y = _layernorm(alpha * r_ref[...] + acc[...], lg_ref[...], lb_ref[...])
        o32_ref[...] = y
        o16_ref[...] = y.astype(BF16)


def _ffn_group_kernel(te_ref, tv_ref, x_ref, w1_ref, w3_ref, w2_ref, o_ref, xb, acc):
    _ffn_body(tv_ref, x_ref, w1_ref, w3_ref, w2_ref, xb, acc)

    @pl.when(pl.program_id(1) == pl.num_programs(1) - 1)
    def _():
        o_ref[...] = acc[...]


def _ffn_specs(tm, tf, d, nf):
    fidx = lambda j, v: j * v + (nf - 1) * (1 - v)
    return [pl.BlockSpec((tm, d), lambda i, j, te, tv: (i, 0)),
            pl.BlockSpec((1, d, tf), lambda i, j, te, tv: (te[i], 0, fidx(j, tv[i]))),
            pl.BlockSpec((1, d, tf), lambda i, j, te, tv: (te[i], 0, fidx(j, tv[i]))),
            pl.BlockSpec((1, tf, d), lambda i, j, te, tv: (te[i], fidx(j, tv[i]), 0))]


def _ffn_dense(x16, x32, w1, w3, w2, lg, lb, *, alpha, tm, tf):
    t, d = x16.shape
    nt, nf = t // tm, w1.shape[2] // tf
    row = lambda i, j, te, tv: (i, 0)
    const = lambda i, j, te, tv: (0, 0)
    te = jnp.zeros((nt,), jnp.int32)
    tv = jnp.ones((nt,), jnp.int32)
    return pl.pallas_call(
        functools.partial(_ffn_dense_kernel, alpha=alpha),
        grid_spec=pltpu.PrefetchScalarGridSpec(
            num_scalar_prefetch=2, grid=(nt, nf),
            in_specs=_ffn_specs(tm, tf, d, nf) + [pl.BlockSpec((tm, d), row), pl.BlockSpec((1, d), const),
                                                   pl.BlockSpec((1, d), const)],
            out_specs=[pl.BlockSpec((tm, d), row)] * 2,
            scratch_shapes=[pltpu.VMEM((tm, d), BF16), pltpu.VMEM((tm, d), F32)]),
        out_shape=[jax.ShapeDtypeStruct((t, d), F32), jax.ShapeDtypeStruct((t, d), BF16)],
        compiler_params=_params("parallel", "arbitrary"),
        name="ffn_dense",
    )(te, tv, x16, w1, w3, w2, x32, lg, lb)


def _ffn_grouped(xs, te, tv, w1, w3, w2, *, tm, tf):
    n, d = xs.shape
    nt, nf = n // tm, w1.shape[2] // tf
    row = lambda i, j, te, tv: (i, 0)
    return pl.pallas_call(
        _ffn_group_kernel,
        grid_spec=pltpu.PrefetchScalarGridSpec(
            num_scalar_prefetch=2, grid=(nt, nf),
            in_specs=_ffn_specs(tm, tf, d, nf),
            out_specs=pl.BlockSpec((tm, d), row),
            scratch_shapes=[pltpu.VMEM((tm, d), BF16), pltpu.VMEM((tm, d), F32)]),
        out_shape=jax.ShapeDtypeStruct((n, d), F32),
        compiler_params=_params("parallel", "arbitrary"),
        name="ffn_grouped",
    )(te, tv, xs, w1, w3, w2)


def _router_kernel(x_ref, wh_ref, wl_ref, idx_ref, wt_ref):
    x = x_ref[...]
    xh = x.astype(BF16)
    xl = (x - xh.astype(F32)).astype(BF16)
    logits = _dot(xh, wh_ref[...]) + _dot(xl, wh_ref[...]) + _dot(xh, wl_ref[...])
    lane = lax.broadcasted_iota(jnp.int32, logits.shape, 1)
    ninf = jnp.float32(-jnp.inf)
    l1 = jnp.where(lane < N_EXPERTS, logits, ninf)
    m1 = jnp.max(l1, axis=-1, keepdims=True)
    i1 = jnp.min(jnp.where(l1 == m1, lane, LANES), axis=-1, keepdims=True)
    l2 = jnp.where(lane == i1, ninf, l1)
    m2 = jnp.max(l2, axis=-1, keepdims=True)
    i2 = jnp.min(jnp.where(l2 == m2, lane, LANES), axis=-1, keepdims=True)
    e = jnp.exp(m2 - m1)
    den = 1.0 + e
    idx_ref[...] = jnp.where(lane == 0, i1, jnp.where(lane == 1, i2, 0))
    wt_ref[...] = jnp.where(lane == 0, 1.0 / den, jnp.where(lane == 1, e / den, 0.0))


def _router(x32, wh, wl, *, tm):
    t, d = x32.shape
    row = lambda i: (i, 0)
    const = lambda i: (0, 0)
    return pl.pallas_call(
        _router_kernel,
        grid=(t // tm,),
        in_specs=[pl.BlockSpec((tm, d), row), pl.BlockSpec((d, LANES), const), pl.BlockSpec((d, LANES), const)],
        out_specs=[pl.BlockSpec((tm, LANES), row)] * 2,
        out_shape=[jax.ShapeDtypeStruct((t, LANES), jnp.int32), jax.ShapeDtypeStruct((t, LANES), F32)],
        compiler_params=_params("parallel"),
        name="router",
    )(x32, wh, wl)


def _row_copy(src_hbm, src_row, dst_ref, dst_row, sem):
    return pltpu.make_async_copy(src_hbm.at[pl.ds(src_row, 1)], dst_ref.at[pl.ds(dst_row, 1)], sem)


def _for_rows(n, fn):
    def group(gi, c):
        for u in range(DMA_UNROLL):
            fn(gi * DMA_UNROLL + u, u % 2)
        return c

    lax.fori_loop(0, n // DMA_UNROLL, group, 0)


def _gather_kernel(src_ref, x_hbm, o_ref, sem):
    tm = o_ref.shape[0]
    base = pl.program_id(0) * tm
    _for_rows(tm, lambda r, par: _row_copy(x_hbm, src_ref[base + r], o_ref, r, sem).start(priority=par))
    _for_rows(tm, lambda r, par: _row_copy(x_hbm, 0, o_ref, r, sem).wait())


def _gather_rows(x32, src, *, tm):
    n = src.shape[0]
    d = x32.shape[1]
    return pl.pallas_call(
        _gather_kernel,
        grid_spec=pltpu.PrefetchScalarGridSpec(
            num_scalar_prefetch=1, grid=(n // tm,),
            in_specs=[pl.BlockSpec(memory_space=pl.ANY)],
            out_specs=pl.BlockSpec((tm, d), lambda i, src: (i, 0)),
            scratch_shapes=[pltpu.SemaphoreType.DMA(())]),
        out_shape=jax.ShapeDtypeStruct((n, d), x32.dtype),
        compiler_params=_params("arbitrary"),
        name="moe_gather",
    )(src, x32)


def _combine_kernel(pos_ref, x_ref, wt_ref, ys_hbm, lg_ref, lb_ref, o32_ref, o16_ref, buf0, buf1, sem, *, alpha):
    tm = x_ref.shape[0]
    base = pl.program_id(0) * tm

    def issue(r, par):
        _row_copy(ys_hbm, pos_ref[2 * (base + r)], buf0, r, sem).start(priority=0)
        _row_copy(ys_hbm, pos_ref[2 * (base + r) + 1], buf1, r, sem).start(priority=1)

    def wait(r, par):
        _row_copy(ys_hbm, 0, buf0, r, sem).wait()
        _row_copy(ys_hbm, 0, buf1, r, sem).wait()

    _for_rows(tm, issue)
    _for_rows(tm, wait)
    wt = wt_ref[...]
    ff = wt[:, 0:1] * buf0[...] + wt[:, 1:2] * buf1[...]
    y = _layernorm(alpha * x_ref[...] + ff, lg_ref[...], lb_ref[...])
    o32_ref[...] = y
    o16_ref[...] = y.astype(BF16)


def _combine(pos, x32, wts, ys, lg, lb, *, alpha, tm):
    t, d = x32.shape
    row = lambda i, pos: (i, 0)
    const = lambda i, pos: (0, 0)
    return pl.pallas_call(
        functools.partial(_combine_kernel, alpha=alpha),
        grid_spec=pltpu.PrefetchScalarGridSpec(
            num_scalar_prefetch=1, grid=(t // tm,),
            in_specs=[pl.BlockSpec((tm, d), row), pl.BlockSpec((tm, LANES), row), pl.BlockSpec(memory_space=pl.ANY),
                      pl.BlockSpec((1, d), const), pl.BlockSpec((1, d), const)],
            out_specs=[pl.BlockSpec((tm, d), row)] * 2,
            scratch_shapes=[pltpu.VMEM((tm, d), F32), pltpu.VMEM((tm, d), F32), pltpu.SemaphoreType.DMA(())]),
        out_shape=[jax.ShapeDtypeStruct((t, d), F32), jax.ShapeDtypeStruct((t, d), BF16)],
        compiler_params=_params("arbitrary"),
        name="moe_combine",
    )(pos, x32, wts, ys, lg, lb)


def _route_meta(idx, tm):
    t = idx.shape[0]
    a = 2 * t
    e = idx.reshape(a)
    onehot = (e[:, None] == jnp.arange(N_EXPERTS, dtype=jnp.int32)[None, :]).astype(jnp.int32)
    csum = jnp.cumsum(onehot, axis=0)
    rank = jnp.sum((csum - onehot) * onehot, axis=1)
    cnt = csum[-1]
    pcnt = ((cnt + tm - 1) // tm) * tm
    pend = jnp.cumsum(pcnt)
    pos = jnp.sum(onehot * (pend - pcnt)[None, :], axis=1) + rank
    n_rows = a + N_EXPERTS * tm
    src = jnp.zeros((n_rows,), jnp.int32).at[pos].set(jnp.arange(a, dtype=jnp.int32) // 2, unique_indices=True)
    start = jnp.arange(n_rows // tm, dtype=jnp.int32) * tm
    te = jnp.minimum(jnp.sum((start[:, None] >= pend[None, :]).astype(jnp.int32), axis=1), N_EXPERTS - 1)
    tv = (start < pend[-1]).astype(jnp.int32)
    return pos.astype(jnp.int32), src, te.astype(jnp.int32), tv


def _prep_w_in(w):
    d = w.shape[0]
    z = lambda n: jnp.zeros((d, n), w.dtype)
    na_q, na_k, na_v = w[:, 0:512] * (HEAD_DIM ** -0.5), w[:, 512:1024], w[:, 1024:1536]
    g_q, g_k, g_v = w[:, 1536:2048], w[:, 2048:2176], w[:, 2176:2304]
    c_q, c_kv, k_r, gate = w[:, 2304:2688], w[:, 2688:2944], w[:, 2944:2976], w[:, 2976:]
    kr_blk = jnp.concatenate([z(MLA_NOPE_DIM), k_r, z(LANES - MLA_NOPE_DIM - MLA_ROPE_DIM)], axis=1)
    out = jnp.concatenate([na_q, na_k, na_v, g_q, g_k, g_v, c_kv, kr_blk, c_q, gate], axis=1).astype(BF16)
    assert out.shape[1] == H_COLS
    return out


def _prep_mla_w(w_uq, w_ukv):
    heads = w_uq.shape[1] // (MLA_NOPE_DIM + MLA_ROPE_DIM)
    wq = w_uq.reshape(MLA_Q_LORA, heads, MLA_NOPE_DIM + MLA_ROPE_DIM)
    wq = jnp.pad(wq, ((0, 0), (0, 0), (0, LANES - MLA_NOPE_DIM - MLA_ROPE_DIM))).reshape(MLA_Q_LORA, heads * LANES)
    wkv = w_ukv.reshape(MLA_KV_LORA, heads, LANES)
    wk = jnp.pad(wkv[:, :, :MLA_NOPE_DIM], ((0, 0), (0, 0), (0, LANES - MLA_NOPE_DIM))).reshape(MLA_KV_LORA, heads * LANES)
    wvt = wkv[:, :, MLA_NOPE_DIM:].reshape(MLA_KV_LORA, heads * HEAD_DIM).T
    return wq.astype(BF16), wk.astype(BF16), wvt.astype(BF16)


def _rope_tables(seq):
    pos = jnp.arange(seq, dtype=jnp.int32)
    half = 16
    inv_freq = ROPE_THETA ** (-jnp.arange(half, dtype=F32) / half)

    def cs(p):
        ang = p.astype(F32)[:, None] * inv_freq[None, :]
        return jnp.cos(ang), jnp.sin(ang)

    cr, sr = cs(pos // GRID_W)
    cc, sc = cs(pos % GRID_W)
    cp, sp = cs(pos)
    one = lambda n: jnp.ones((seq, n), F32)
    zero = lambda n: jnp.zeros((seq, n), F32)
    g_cos = jnp.tile(jnp.concatenate([cr, cr, cc, cc], axis=1), (1, 2))
    g_sin = jnp.tile(jnp.concatenate([-sr, sr, -sc, sc], axis=1), (1, 2))
    m_cos = jnp.concatenate([one(MLA_NOPE_DIM), cp, cp, one(32)], axis=1)
    m_sin = jnp.concatenate([zero(MLA_NOPE_DIM), -sp, sp, zero(32)], axis=1)
    return g_cos, g_sin, m_cos, m_sin


def kernel(x_prompt, x_sample, w_in, na_rpb, gqa_q_norm, gqa_k_norm, mla_q_norm, mla_w_uq, mla_kv_norm, mla_w_ukv, w_branch_a, w_branch_b, w_branch_c, w_out, ln1_g, ln1_b, ln2_g, ln2_b, ffn_w1, ffn_w3, ffn_w2, moe_router, moe_w1, moe_w3, moe_w2):
    depth = w_in.shape[0]
    alpha = float((2 * depth) ** 0.25)
    groups = [(x_prompt.shape[0], x_prompt.shape[1]), (x_sample.shape[0], x_sample.shape[1])]
    d = x_prompt.shape[2]
    assert d == D_MODEL
    x32 = jnp.concatenate([x_prompt.reshape(-1, d), x_sample.reshape(-1, d)], axis=0)
    t = x32.shape[0]
    x16 = x32.astype(BF16)

    tabs = [_rope_tables(s) for _, s in groups]
    g_cos, g_sin, m_cos, m_sin = [jnp.concatenate([jnp.tile(tabs[g][n], (groups[g][0], 1)) for g in range(2)], axis=0)
                                  for n in range(4)]
    blk = np.kron(np.eye(2), np.ones((HEAD_DIM, HEAD_DIM)))
    ones_bd = jnp.asarray(blk, BF16)

    tm_proj = _pick(t, 2048)
    tm_tok = _pick(t, 512, LANES)
    tm_moe = _pick(t, 512)
    row2 = lambda v: v.reshape(1, -1).astype(F32)

    for i in range(depth):
        h = _matmul(x16, _prep_w_in(w_in[i]), tm=tm_proj, tn=512)
        qg = row2(jnp.tile(gqa_q_norm[i] * (HEAD_DIM ** -0.5 * LOG2E), 2))
        kg = row2(jnp.tile(gqa_k_norm[i], 2))
        gq, gkd, gvt = _gqa_prep(h, qg, kg, g_cos, g_sin, ones_bd, tm=tm_tok)
        wq, wk, wvt = _prep_mla_w(mla_w_uq[i], mla_w_ukv[i])
        mq, mk, mvt = _mla_prep(h, row2(mla_q_norm[i]), row2(mla_kv_norm[i]), wq, wk, wvt, m_cos, m_sin, tm=tm_tok)
        bias = _na_bias_planes(na_rpb[i])
        ya, ybt, yct = [], [], []
        off = 0
        for batch, seq in groups:
            tq = _pick(seq, 256, LANES)
            ya.append(_na_attn(h, bias, tok_off=off, batch=batch, seq=seq))
            n_sub = seq // SUB_KEYS
            ybt.append(_gqa_attn(gq, gkd, gvt, tok_off=off, batch=batch, seq=seq, tq=tq,
                                 groups_per_step=2 if GQA_GROUP * n_sub < TILES_PER_BODY else 1))
            yct.append(_mla_attn(mq, mk, mvt, tok_off=off, batch=batch, seq=seq, tq=tq,
                                 heads_per_step=min(8, max(2, TILES_PER_BODY // n_sub))))
            off += batch * seq
        x32, x16 = _merge(ya, ybt, yct, h, x32, w_branch_a[i].astype(BF16), w_branch_b[i].astype(BF16),
                          w_branch_c[i].astype(BF16), w_out[i].astype(BF16), row2(ln1_g[i]), row2(ln1_b[i]),
                          alpha=alpha, tm=tm_tok)
        j = i // 2
        if i % 2 == 0:
            f = ffn_w1.shape[2]
            x32, x16 = _ffn_dense(x16, x32, ffn_w1[j:j + 1].astype(BF16), ffn_w3[j:j + 1].astype(BF16),
                                  ffn_w2[j:j + 1].astype(BF16), row2(ln2_g[i]), row2(ln2_b[i]),
                                  alpha=alpha, tm=_pick(t, 256, LANES), tf=_pick(f, 2816, 2 * LANES))
        else:
            rw = jnp.pad(moe_router[j], ((0, 0), (0, LANES - N_EXPERTS)))
            rw_hi = rw.astype(BF16)
            rw_lo = (rw - rw_hi.astype(F32)).astype(BF16)
            idx, wts = _router(x32, rw_hi, rw_lo, tm=tm_tok)
            pos, src, te, tv = _route_meta(idx[:, :2], tm_moe)
            xs = _gather_rows(x32, src, tm=tm_moe)
            f = moe_w1.shape[3]
            ys = _ffn_grouped(xs, te, tv, moe_w1[j].astype(BF16), moe_w3[j].astype(BF16),
                              moe_w2[j].astype(BF16), tm=tm_moe, tf=_pick(f, 1792, 2 * LANES))
            x32, x16 = _combine(pos, x32, wts, ys, row2(ln2_g[i]), row2(ln2_b[i]), alpha=alpha, tm=tm_moe)

    tp = groups[0][0] * groups[0][1]
    return (x32[:tp].reshape(x_prompt.shape), x32[tp:].reshape(x_sample.shape))
```

```python
import functools
import math

import numpy as np
import jax
import jax.numpy as jnp
from jax import lax
from jax.experimental import pallas as pl
from jax.experimental.pallas import tpu as pltpu

F32 = jnp.float32
BF16 = jnp.bfloat16

D_MODEL = 1024
GRID_W = 64
HEAD_DIM = 64
NA_WIN_H = 8
NA_WIN_W = 16
MLA_Q_LORA = 384
MLA_KV_LORA = 256
MLA_NOPE_DIM = 64
MLA_ROPE_DIM = 32
N_EXPERTS = 8
GQA_GROUP = 4
ROPE_THETA = 10000.0
RMS_EPS = 1e-6
LN_EPS = 1e-5
NEG_INF = -1e30
LOG2E = math.log2(math.e)

LANES = 128
BF16_ROWS = 16
ROWS_PER_STEP = 8
DMA_UNROLL = 8
SUB_KEYS = 256
QK_AHEAD = 6
TILES_PER_BODY = 64
NA_AHEAD = 2

C_NAQ, C_NAK, C_NAV = 0, 512, 1024
C_GQ, C_GK, C_GV = 1536, 2048, 2176
C_CKV, C_KR, C_CQ, C_GATE = 2304, 2560, 2688, 3072
H_COLS = 6144

VMEM_LIMIT = 56 * 1024 * 1024


def _params(*sem):
    return pltpu.CompilerParams(dimension_semantics=sem, vmem_limit_bytes=VMEM_LIMIT)


def _pick(n, pref, mult=8):
    t = min(pref, n)
    while t > mult and (n % t or t % mult):
        t -= mult
    assert n % t == 0, (n, pref)
    return t


def _dot(a, b):
    return jnp.dot(a, b, preferred_element_type=F32)


def _dot_nt(a, b):
    return lax.dot_general(a, b, (((1,), (1,)), ((), ())), preferred_element_type=F32)


def _dot_tn(a, b):
    return lax.dot_general(a, b, (((0,), (0,)), ((), ())), preferred_element_type=F32)


def _layernorm(z, g, b):
    mu = jnp.mean(z, axis=-1, keepdims=True)
    zc = z - mu
    var = jnp.mean(zc * zc, axis=-1, keepdims=True)
    return zc * lax.rsqrt(var + LN_EPS) * g + b


def _mm_kernel(x_ref, w_ref, o_ref):
    o_ref[...] = _dot(x_ref[...], w_ref[...]).astype(o_ref.dtype)


def _matmul(x, w, *, tm, tn, out_dtype=BF16):
    m, k = x.shape
    n = w.shape[1]
    return pl.pallas_call(
        _mm_kernel,
        grid=(m // tm, n // tn),
        in_specs=[pl.BlockSpec((tm, k), lambda i, j: (i, 0)),
                  pl.BlockSpec((k, tn), lambda i, j: (0, j))],
        out_specs=pl.BlockSpec((tm, tn), lambda i, j: (i, j)),
        out_shape=jax.ShapeDtypeStruct((m, n), out_dtype),
        compiler_params=_params("parallel", "parallel"),
        name="proj_in",
    )(x, w)


def _rope_apply(y, cos, sin_signed, lane):
    partner = jnp.where(lane % 32 < 16, pltpu.roll(y, LANES - 16, 1), pltpu.roll(y, 16, 1))
    return y * cos + partner * sin_signed


def _group_sumsq(x, ones_bd):
    s = x * x
    s_hi = s.astype(BF16)
    s_lo = (s - s_hi.astype(F32)).astype(BF16)
    return _dot(s_hi, ones_bd) + _dot(s_lo, ones_bd)


def _gqa_prep_kernel(q_ref, k_ref, v_ref, qg_ref, kg_ref, c_ref, s_ref, ones_ref, qo_ref, kd_ref, vt_ref):
    tm = q_ref.shape[0]
    lane = lax.broadcasted_iota(jnp.int32, (tm, LANES), 1)
    cos, sin_signed, ones_bd = c_ref[...], s_ref[...], ones_ref[...]

    def norm_rope(x, gain):
        y = x * lax.rsqrt(_group_sumsq(x, ones_bd) * (1.0 / HEAD_DIM) + RMS_EPS) * gain
        return _rope_apply(y, cos, sin_signed, lane)

    for j in range(q_ref.shape[1] // LANES):
        sl = slice(j * LANES, (j + 1) * LANES)
        qo_ref[sl, :] = norm_rope(q_ref[:, sl].astype(F32), qg_ref[...]).T.astype(BF16)
    low = lane < HEAD_DIM
    k = norm_rope(k_ref[...].astype(F32), kg_ref[...])
    k_sw = pltpu.roll(k, HEAD_DIM, 1)
    kd_ref[:, 0:LANES] = jnp.where(low, k, k_sw).astype(BF16)
    kd_ref[:, LANES:2 * LANES] = jnp.where(low, k_sw, k).astype(BF16)
    vt_ref[...] = v_ref[...].astype(F32).T.astype(BF16)


def _gqa_prep(h, qg, kg, cos, sin_signed, ones_bd, *, tm):
    t = h.shape[0]
    row = lambda i: (i, 0)
    const = lambda i: (0, 0)
    return pl.pallas_call(
        _gqa_prep_kernel,
        grid=(t // tm,),
        in_specs=[pl.BlockSpec((tm, 512), lambda i: (i, C_GQ // 512)),
                  pl.BlockSpec((tm, LANES), lambda i: (i, C_GK // LANES)),
                  pl.BlockSpec((tm, LANES), lambda i: (i, C_GV // LANES)),
                  pl.BlockSpec((1, LANES), const), pl.BlockSpec((1, LANES), const),
                  pl.BlockSpec((tm, LANES), row), pl.BlockSpec((tm, LANES), row),
                  pl.BlockSpec((LANES, LANES), const)],
        out_specs=[pl.BlockSpec((512, tm), lambda i: (0, i)), pl.BlockSpec((tm, 256), row),
                   pl.BlockSpec((LANES, tm), lambda i: (0, i))],
        out_shape=[jax.ShapeDtypeStruct((512, t), BF16), jax.ShapeDtypeStruct((t, 256), BF16),
                   jax.ShapeDtypeStruct((LANES, t), BF16)],
        compiler_params=_params("parallel"),
        name="gqa_prep",
    )(h, h, h, qg, kg, cos, sin_signed, ones_bd)


def _mla_prep_kernel(cq_ref, ckv_ref, kr_ref, qn_ref, kvn_ref, wq_ref, wk_ref, wvt_ref, c_ref, s_ref,
                     q_ref, k_ref, vt_ref, *, scale):
    tm = cq_ref.shape[0]
    lane = lax.broadcasted_iota(jnp.int32, (tm, LANES), 1)
    cos, sin_signed = c_ref[...], s_ref[...]

    def rms(x, g):
        return x * lax.rsqrt(jnp.mean(x * x, axis=-1, keepdims=True) + RMS_EPS) * g

    cq = rms(cq_ref[...].astype(F32), qn_ref[...]).astype(BF16)
    ckv = rms(ckv_ref[...].astype(F32), kvn_ref[...]).astype(BF16)
    q = _dot(cq, wq_ref[...])
    k = _dot(ckv, wk_ref[...])
    vt_ref[...] = _dot_nt(wvt_ref[...], ckv).astype(BF16)
    k_rope = _rope_apply(kr_ref[...].astype(F32), cos, sin_signed, lane)
    for hd in range(q.shape[1] // LANES):
        sl = slice(hd * LANES, (hd + 1) * LANES)
        q_ref[sl, :] = (_rope_apply(q[:, sl], cos, sin_signed, lane) * scale).T.astype(BF16)
        k_ref[:, sl] = (k[:, sl] + k_rope).astype(BF16)


def _mla_prep(h, qn, kvn, wq, wk, wvt, cos, sin_signed, *, tm):
    t = h.shape[0]
    row = lambda i: (i, 0)
    const = lambda i: (0, 0)
    scale = float((MLA_NOPE_DIM + MLA_ROPE_DIM) ** -0.5 * LOG2E)
    return pl.pallas_call(
        functools.partial(_mla_prep_kernel, scale=scale),
        grid=(t // tm,),
        in_specs=[pl.BlockSpec((tm, MLA_Q_LORA), lambda i: (i, C_CQ // MLA_Q_LORA)),
                  pl.BlockSpec((tm, MLA_KV_LORA), lambda i: (i, C_CKV // MLA_KV_LORA)),
                  pl.BlockSpec((tm, LANES), lambda i: (i, C_KR // LANES)),
                  pl.BlockSpec((1, MLA_Q_LORA), const), pl.BlockSpec((1, MLA_KV_LORA), const),
                  pl.BlockSpec(wq.shape, const), pl.BlockSpec(wk.shape, const), pl.BlockSpec(wvt.shape, const),
                  pl.BlockSpec((tm, LANES), row), pl.BlockSpec((tm, LANES), row)],
        out_specs=[pl.BlockSpec((1024, tm), lambda i: (0, i)), pl.BlockSpec((tm, 1024), row),
                   pl.BlockSpec((512, tm), lambda i: (0, i))],
        out_shape=[jax.ShapeDtypeStruct((1024, t), BF16), jax.ShapeDtypeStruct((t, 1024), BF16),
                   jax.ShapeDtypeStruct((512, t), BF16)],
        compiler_params=_params("parallel"),
        name="mla_prep",
    )(h, h, h, qn, kvn, wq, wk, wvt, cos, sin_signed)


def _flash_t(q_ops, kv_fn, n_chunks, n_sub, tq):
    init = tuple((jnp.full((1, tq), NEG_INF, F32), jnp.zeros((HEAD_DIM + BF16_ROWS, tq), F32)) for _ in q_ops)

    def body(c, carry):
        carry = list(carry)
        tiles = [(s, hd) for s in range(n_sub) for hd in range(len(q_ops))]
        kvs = {}

        def scores(s, hd):
            if s not in kvs:
                kvs[s] = kv_fn(c, s)
            return _dot(kvs[s][hd][0], q_ops[hd])

        pending = [scores(*tl) for tl in tiles[:QK_AHEAD]]
        for n, (s, hd) in enumerate(tiles):
            if n + QK_AHEAD < len(tiles):
                pending.append(scores(*tiles[n + QK_AHEAD]))
            st = pending.pop(0)
            m, acc = carry[hd]
            m_new = jnp.maximum(m, jnp.max(st, axis=0, keepdims=True))
            alpha = jnp.exp2(m - m_new)
            pt = jnp.exp2(st - m_new).astype(BF16)
            carry[hd] = (m_new, acc * alpha + _dot(kvs[s][hd][1], pt))
        return tuple(carry)

    res = lax.fori_loop(0, n_chunks, body, init)
    return [acc[0:HEAD_DIM] / acc[HEAD_DIM:HEAD_DIM + 1] for _, acc in res]


def _gqa_attn_kernel(q_ref, kd_ref, vt_ref, o_ref, *, tk):
    tq = q_ref.shape[1]
    n_heads = q_ref.shape[0] // HEAD_DIM
    zero = jnp.zeros((HEAD_DIM, tq), BF16)
    q_ops = [jnp.concatenate([q_ref[hd * HEAD_DIM:(hd + 1) * HEAD_DIM, :], zero], axis=0) for hd in range(n_heads)]
    ones = jnp.ones((BF16_ROWS, SUB_KEYS), BF16)

    def kv_fn(c, s):
        rows = pl.ds(pl.multiple_of(c * tk + s * SUB_KEYS, SUB_KEYS), SUB_KEYS)
        per_group = [(kd_ref[rows, g * LANES:(g + 1) * LANES],
                      jnp.concatenate([vt_ref[g * HEAD_DIM:(g + 1) * HEAD_DIM, rows], ones], axis=0))
                     for g in range(n_heads // GQA_GROUP)]
        return [per_group[hd // GQA_GROUP] for hd in range(n_heads)]

    outs = _flash_t(q_ops, kv_fn, kd_ref.shape[0] // tk, tk // SUB_KEYS, tq)
    for hd, o in enumerate(outs):
        o_ref[hd * HEAD_DIM:(hd + 1) * HEAD_DIM, :] = o.astype(o_ref.dtype)


def _gqa_attn(q, kd, vt, *, tok_off, batch, seq, tq, groups_per_step):
    n_groups = vt.shape[0] // HEAD_DIM
    gps = groups_per_step
    tk = _pick(seq, TILES_PER_BODY // (gps * GQA_GROUP) * SUB_KEYS, SUB_KEYS)
    assert tok_off % seq == 0 and seq % tq == 0 and n_groups % gps == 0
    nq = seq // tq
    return pl.pallas_call(
        functools.partial(_gqa_attn_kernel, tk=tk),
        grid=(batch, n_groups // gps, nq),
        in_specs=[pl.BlockSpec((gps * GQA_GROUP * HEAD_DIM, tq), lambda b, g, i: (g, tok_off // tq + b * nq + i)),
                  pl.BlockSpec((seq, gps * LANES), lambda b, g, i: (tok_off // seq + b, g)),
                  pl.BlockSpec((gps * HEAD_DIM, seq), lambda b, g, i: (g, tok_off // seq + b))],
        out_specs=pl.BlockSpec((gps * GQA_GROUP * HEAD_DIM, tq), lambda b, g, i: (g, b * nq + i)),
        out_shape=jax.ShapeDtypeStruct((512, batch * seq), BF16),
        compiler_params=_params("parallel", "parallel", "parallel"),
        name="gqa_attn",
    )(q, kd, vt)


def _mla_attn_kernel(q_ref, k_ref, vt_ref, o_ref, *, tk):
    tq = q_ref.shape[1]
    n_heads = q_ref.shape[0] // LANES
    q_ops = [q_ref[hd * LANES:(hd + 1) * LANES, :] for hd in range(n_heads)]
    ones = jnp.ones((BF16_ROWS, SUB_KEYS), BF16)

    def kv_fn(c, s):
        rows = pl.ds(pl.multiple_of(c * tk + s * SUB_KEYS, SUB_KEYS), SUB_KEYS)
        return [(k_ref[rows, hd * LANES:(hd + 1) * LANES],
                 jnp.concatenate([vt_ref[hd * HEAD_DIM:(hd + 1) * HEAD_DIM, rows], ones], axis=0))
                for hd in range(n_heads)]

    outs = _flash_t(q_ops, kv_fn, k_ref.shape[0] // tk, tk // SUB_KEYS, tq)
    for hd, o in enumerate(outs):
        o_ref[hd * HEAD_DIM:(hd + 1) * HEAD_DIM, :] = o.astype(o_ref.dtype)


def _mla_attn(q, k, vt, *, tok_off, batch, seq, tq, heads_per_step):
    n_heads = vt.shape[0] // HEAD_DIM
    hps = heads_per_step
    tk = _pick(seq, TILES_PER_BODY // hps * SUB_KEYS, SUB_KEYS)
    assert tok_off % seq == 0 and seq % tq == 0 and n_heads % hps == 0
    nq = seq // tq
    return pl.pallas_call(
        functools.partial(_mla_attn_kernel, tk=tk),
        grid=(batch, n_heads // hps, nq),
        in_specs=[pl.BlockSpec((hps * LANES, tq), lambda b, p, i: (p, tok_off // tq + b * nq + i)),
                  pl.BlockSpec((seq, hps * LANES), lambda b, p, i: (tok_off // seq + b, p)),
                  pl.BlockSpec((hps * HEAD_DIM, seq), lambda b, p, i: (p, tok_off // seq + b))],
        out_specs=pl.BlockSpec((hps * HEAD_DIM, tq), lambda b, p, i: (p, b * nq + i)),
        out_shape=jax.ShapeDtypeStruct((512, batch * seq), BF16),
        compiler_params=_params("parallel", "parallel", "parallel"),
        name="mla_attn",
    )(q, k, vt)


NA_PAIR_WIN = (NA_WIN_H + 2) * GRID_W
NA_REL_ROWS = 2 * NA_WIN_H - 1


def _na_bias_planes(rpb):
    c = np.arange(GRID_W)
    kc = np.arange(GRID_W)
    c0 = np.clip(c - NA_WIN_W // 2, 0, GRID_W - NA_WIN_W)
    ok = (kc[None, :] >= c0[:, None]) & (kc[None, :] < c0[:, None] + NA_WIN_W)
    pad = GRID_W - NA_WIN_W
    padded = jnp.pad(rpb.astype(F32), ((0, 0), (0, 0), (pad, pad)))
    planes = jnp.stack([padded[:, :, GRID_W - 1 - ci:2 * GRID_W - 1 - ci] for ci in range(GRID_W)], axis=2)
    planes = jnp.swapaxes(jnp.where(ok[None, None], planes, NEG_INF), 2, 3)
    n_pairs = rpb.shape[0] // 2
    planes = planes.reshape(n_pairs, 2, NA_REL_ROWS, GRID_W, GRID_W)
    planes = jnp.concatenate([planes[:, 0], planes[:, 1]], axis=-1)
    return jnp.pad(planes, ((0, 0), (0, 1), (0, 0), (0, 0)), constant_values=NEG_INF)


def _na_kernel(q_ref, kp_ref, kc_ref, kn_ref, vp_ref, vc_ref, vn_ref, b_ref, o_ref, kbuf, vtbuf, *, rows):
    blk = ROWS_PER_STEP * GRID_W
    j = pl.program_id(2)
    for n, (kr, vr) in enumerate(((kp_ref, vp_ref), (kc_ref, vc_ref), (kn_ref, vn_ref))):
        kbuf[n * blk:(n + 1) * blk, :] = kr[...]
        vtbuf[:, n * blk:(n + 1) * blk] = vr[...].astype(F32).T.astype(BF16)
    low = lax.broadcasted_iota(jnp.int32, (GRID_W, LANES), 1) < HEAD_DIM
    ones = jnp.ones((BF16_ROWS, NA_PAIR_WIN), BF16)
    n_tiles = ROWS_PER_STEP // 2

    def scores(ip):
        r = j * ROWS_PER_STEP + 2 * ip
        r0 = jnp.clip(r - NA_WIN_H // 2, 0, rows - NA_WIN_H)
        r0b = jnp.clip(r + 1 - NA_WIN_H // 2, 0, rows - NA_WIN_H)
        off = pl.multiple_of((r0 - j * ROWS_PER_STEP + ROWS_PER_STEP) * GRID_W, 2 * GRID_W)
        bias_rows = []
        for t in range(NA_PAIR_WIN // GRID_W):
            ia = r0 + t - r + (NA_WIN_H - 1) if t < NA_WIN_H else NA_REL_ROWS
            tb = r0 + t - r0b
            ib = jnp.where((tb >= 0) & (tb < NA_WIN_H), r0 + t - (r + 1) + (NA_WIN_H - 1), NA_REL_ROWS)
            bias_rows.append(jnp.concatenate([b_ref[0, ia], b_ref[0, ib]], axis=1))
        bias = jnp.concatenate(bias_rows, axis=0)
        q4 = []
        for i in (2 * ip, 2 * ip + 1):
            q = q_ref[i * GRID_W:(i + 1) * GRID_W, :]
            zero = jnp.zeros_like(q)
            q4 += [jnp.where(low, q, zero), jnp.where(low, zero, q)]
        st = _dot_nt(kbuf[pl.ds(off, NA_PAIR_WIN), :], jnp.concatenate(q4, axis=0)) + bias
        return off, st

    pending = [scores(ip) for ip in range(min(NA_AHEAD, n_tiles))]
    for ip in range(n_tiles):
        if ip + NA_AHEAD < n_tiles:
            pending.append(scores(ip + NA_AHEAD))
        off, st = pending.pop(0)
        pt = jnp.exp(st - jnp.max(st, axis=0, keepdims=True)).astype(BF16)
        v_aug = jnp.concatenate([vtbuf[:, pl.ds(off, NA_PAIR_WIN)], ones], axis=0)
        o = _dot(v_aug, pt)
        t = (o[0:LANES] / o[LANES:LANES + 1]).T
        for n in range(2):
            rows_out = slice((2 * ip + n) * GRID_W, (2 * ip + n + 1) * GRID_W)
            o_ref[rows_out, :] = jnp.where(low, t[2 * n * GRID_W:(2 * n + 1) * GRID_W],
                                           t[(2 * n + 1) * GRID_W:(2 * n + 2) * GRID_W]).astype(o_ref.dtype)


def _na_attn(h, bias, *, tok_off, batch, seq):
    rows = seq // GRID_W
    blk = ROWS_PER_STEP * GRID_W
    assert rows % ROWS_PER_STEP == 0 and rows >= NA_WIN_H and tok_off % blk == 0
    nb = rows // ROWS_PER_STEP
    n_pairs = bias.shape[0]
    base = tok_off // blk

    def tokmap(col0, shift):
        return lambda b, p, j: (base + b * nb + jnp.clip(j + shift, 0, nb - 1), col0 // LANES + p)

    kv_specs = [pl.BlockSpec((blk, LANES), tokmap(c0, s)) for c0 in (C_NAK, C_NAV) for s in (-1, 0, 1)]
    return pl.pallas_call(
        functools.partial(_na_kernel, rows=rows),
        grid=(batch, n_pairs, nb),
        in_specs=[pl.BlockSpec((blk, LANES), tokmap(C_NAQ, 0))] + kv_specs
                 + [pl.BlockSpec((1, NA_REL_ROWS + 1, GRID_W, LANES), lambda b, p, j: (p, 0, 0, 0))],
        out_specs=pl.BlockSpec((blk, LANES), lambda b, p, j: (b * nb + j, p)),
        out_shape=jax.ShapeDtypeStruct((batch * seq, 512), BF16),
        scratch_shapes=[pltpu.VMEM((3 * blk, LANES), BF16), pltpu.VMEM((LANES, 3 * blk), BF16)],
        compiler_params=_params("parallel", "parallel", "parallel"),
        name="na_attn",
    )(h, h, h, h, h, h, h, bias)


def _merge_kernel(ya0_ref, ya1_ref, ybt0_ref, ybt1_ref, yct0_ref, yct1_ref, g0_ref, g1_ref, g2_ref, x_ref,
                  wa_ref, wb_ref, wc_ref, wo_ref, lg_ref, lb_ref, o32_ref, o16_ref, *, alpha, tiles0):
    def gate(g_ref):
        return jax.nn.sigmoid(g_ref[...].astype(F32))

    in_group0 = pl.program_id(0) < tiles0
    pick = lambda r0, r1: jnp.where(in_group0, r0[...], r1[...])
    merged = (gate(g0_ref) * _dot(pick(ya0_ref, ya1_ref), wa_ref[...])
              + gate(g1_ref) * _dot_tn(pick(ybt0_ref, ybt1_ref), wb_ref[...])
              + gate(g2_ref) * _dot_tn(pick(yct0_ref, yct1_ref), wc_ref[...]))
    mix = _dot(merged.astype(BF16), wo_ref[...])
    y = _layernorm(alpha * x_ref[...] + mix, lg_ref[...], lb_ref[...])
    o32_ref[...] = y
    o16_ref[...] = y.astype(BF16)


def _merge(ya, ybt, yct, h, x, wa, wb, wc, wo, lg, lb, *, alpha, tm):
    t = x.shape[0]
    assert ya[0].shape[0] % tm == 0 and ya[0].shape[0] + ya[1].shape[0] == t
    tiles0 = ya[0].shape[0] // tm
    tiles1 = t // tm - tiles0
    row = lambda i: (i, 0)
    const = lambda i: (0, 0)
    idx0 = lambda i: jnp.minimum(i, tiles0 - 1)
    idx1 = lambda i: jnp.clip(i - tiles0, 0, tiles1 - 1)
    tok = lambda idx: pl.BlockSpec((tm, 512), lambda i: (idx(i), 0))
    feat = lambda idx: pl.BlockSpec((512, tm), lambda i: (0, idx(i)))
    gate = lambda n: pl.BlockSpec((tm, D_MODEL), lambda i: (i, C_GATE // D_MODEL + n))
    return pl.pallas_call(
        functools.partial(_merge_kernel, alpha=alpha, tiles0=tiles0),
        grid=(t // tm,),
        in_specs=[tok(idx0), tok(idx1), feat(idx0), feat(idx1), feat(idx0), feat(idx1),
                  gate(0), gate(1), gate(2), pl.BlockSpec((tm, D_MODEL), row)]
                 + [pl.BlockSpec((512, D_MODEL), const)] * 3 + [pl.BlockSpec((D_MODEL, D_MODEL), const)]
                 + [pl.BlockSpec((1, D_MODEL), const)] * 2,
        out_specs=[pl.BlockSpec((tm, D_MODEL), row)] * 2,
        out_shape=[jax.ShapeDtypeStruct((t, D_MODEL), F32), jax.ShapeDtypeStruct((t, D_MODEL), BF16)],
        compiler_params=_params("parallel"),
        name="merge_ln1",
    )(ya[0], ya[1], ybt[0], ybt[1], yct[0], yct[1], h, h, h, x, wa, wb, wc, wo, lg, lb)


def _ffn_body(tv_ref, x_ref, w1_ref, w3_ref, w2_ref, xb, acc):
    i, j = pl.program_id(0), pl.program_id(1)

    @pl.when(j == 0)
    def _():
        acc[...] = jnp.zeros_like(acc)
        xb[...] = x_ref[...].astype(BF16)

    @pl.when(tv_ref[i] != 0)
    def _():
        x = xb[...]
        a = _dot(x, w1_ref[0])
        b = _dot(x, w3_ref[0])
        mid = (a * jax.nn.sigmoid(a)) * b
        acc[...] += _dot(mid.astype(BF16), w2_ref[0])


def _ffn_dense_kernel(te_ref, tv_ref, x_ref, w1_ref, w3_ref, w2_ref, r_ref, lg_ref, lb_ref, o32_ref, o16_ref,
                      xb, acc, *, alpha):
    _ffn_body(tv_ref, x_ref, w1_ref, w3_ref, w2_ref, xb, acc)

    @pl.when(pl.program_id(1) == pl.num_programs(1) - 1)
    def _():
        y = _layernorm(alpha * r_ref[...] + acc[...], lg_ref[...], lb_ref[...])
        o32_ref[...] = y
        o16_ref[...] = y.astype(BF16)


def _ffn_group_kernel(te_ref, tv_ref, x_ref, w1_ref, w3_ref, w2_ref, o_ref, xb, acc):
    _ffn_body(tv_ref, x_ref, w1_ref, w3_ref, w2_ref, xb, acc)

    @pl.when(pl.program_id(1) == pl.num_programs(1) - 1)
    def _():
        o_ref[...] = acc[...]


def _ffn_specs(tm, tf, d, nf):
    fidx = lambda j, v: j * v + (nf - 1) * (1 - v)
    return [pl.BlockSpec((tm, d), lambda i, j, te, tv: (i, 0)),
            pl.BlockSpec((1, d, tf), lambda i, j, te, tv: (te[i], 0, fidx(j, tv[i]))),
            pl.BlockSpec((1, d, tf), lambda i, j, te, tv: (te[i], 0, fidx(j, tv[i]))),
            pl.BlockSpec((1, tf, d), lambda i, j, te, tv: (te[i], fidx(j, tv[i]), 0))]


def _ffn_dense(x16, x32, w1, w3, w2, lg, lb, *, alpha, tm, tf):
    t, d = x16.shape
    nt, nf = t // tm, w1.shape[2] // tf
    row = lambda i, j, te, tv: (i, 0)
    const = lambda i, j, te, tv: (0, 0)
    te = jnp.zeros((nt,), jnp.int32)
    tv = jnp.ones((nt,), jnp.int32)
    return pl.pallas_call(
        functools.partial(_ffn_dense_kernel, alpha=alpha),
        grid_spec=pltpu.PrefetchScalarGridSpec(
            num_scalar_prefetch=2, grid=(nt, nf),
            in_specs=_ffn_specs(tm, tf, d, nf) + [pl.BlockSpec((tm, d), row), pl.BlockSpec((1, d), const),
                                                   pl.BlockSpec((1, d), const)],
            out_specs=[pl.BlockSpec((tm, d), row)] * 2,
            scratch_shapes=[pltpu.VMEM((tm, d), BF16), pltpu.VMEM((tm, d), F32)]),
        out_shape=[jax.ShapeDtypeStruct((t, d), F32), jax.ShapeDtypeStruct((t, d), BF16)],
        compiler_params=_params("parallel", "arbitrary"),
        name="ffn_dense",
    )(te, tv, x16, w1, w3, w2, x32, lg, lb)


def _ffn_grouped(xs, te, tv, w1, w3, w2, *, tm, tf):
    n, d = xs.shape
    nt, nf = n // tm, w1.shape[2] // tf
    row = lambda i, j, te, tv: (i, 0)
    return pl.pallas_call(
        _ffn_group_kernel,
        grid_spec=pltpu.PrefetchScalarGridSpec(
            num_scalar_prefetch=2, grid=(nt, nf),
            in_specs=_ffn_specs(tm, tf, d, nf),
            out_specs=pl.BlockSpec((tm, d), row),
            scratch_shapes=[pltpu.VMEM((tm, d), BF16), pltpu.VMEM((tm, d), F32)]),
        out_shape=jax.ShapeDtypeStruct((n, d), F32),
        compiler_params=_params("parallel", "arbitrary"),
        name="ffn_grouped",
    )(te, tv, xs, w1, w3, w2)


def _router_kernel(x_ref, wh_ref, wl_ref, idx_ref, wt_ref):
    x = x_ref[...]
    xh = x.astype(BF16)
    xl = (x - xh.astype(F32)).astype(BF16)
    logits = _dot(xh, wh_ref[...]) + _dot(xl, wh_ref[...]) + _dot(xh, wl_ref[...])
    lane = lax.broadcasted_iota(jnp.int32, logits.shape, 1)
    ninf = jnp.float32(-jnp.inf)
    l1 = jnp.where(lane < N_EXPERTS, logits, ninf)
    m1 = jnp.max(l1, axis=-1, keepdims=True)
    i1 = jnp.min(jnp.where(l1 == m1, lane, LANES), axis=-1, keepdims=True)
    l2 = jnp.where(lane == i1, ninf, l1)
    m2 = jnp.max(l2, axis=-1, keepdims=True)
    i2 = jnp.min(jnp.where(l2 == m2, lane, LANES), axis=-1, keepdims=True)
    e = jnp.exp(m2 - m1)
    den = 1.0 + e
    idx_ref[...] = jnp.where(lane == 0, i1, jnp.where(lane == 1, i2, 0))
    wt_ref[...] = jnp.where(lane == 0, 1.0 / den, jnp.where(lane == 1, e / den, 0.0))


def _router(x32, wh, wl, *, tm):
    t, d = x32.shape
    row = lambda i: (i, 0)
    const = lambda i: (0, 0)
    return pl.pallas_call(
        _router_kernel,
        grid=(t // tm,),
        in_specs=[pl.BlockSpec((tm, d), row), pl.BlockSpec((d, LANES), const), pl.BlockSpec((d, LANES), const)],
        out_specs=[pl.BlockSpec((tm, LANES), row)] * 2,
        out_shape=[jax.ShapeDtypeStruct((t, LANES), jnp.int32), jax.ShapeDtypeStruct((t, LANES), F32)],
        compiler_params=_params("parallel"),
        name="router",
    )(x32, wh, wl)


def _row_copy(src_hbm, src_row, dst_ref, dst_row, sem):
    return pltpu.make_async_copy(src_hbm.at[pl.ds(src_row, 1)], dst_ref.at[pl.ds(dst_row, 1)], sem)


def _for_rows(n, fn):
    def group(gi, c):
        for u in range(DMA_UNROLL):
            fn(gi * DMA_UNROLL + u, u % 2)
        return c

    lax.fori_loop(0, n // DMA_UNROLL, group, 0)


def _gather_kernel(src_ref, x_hbm, o_ref, sem):
    tm = o_ref.shape[0]
    base = pl.program_id(0) * tm
    _for_rows(tm, lambda r, par: _row_copy(x_hbm, src_ref[base + r], o_ref, r, sem).start(priority=par))
    _for_rows(tm, lambda r, par: _row_copy(x_hbm, 0, o_ref, r, sem).wait())


def _gather_rows(x32, src, *, tm):
    n = src.shape[0]
    d = x32.shape[1]
    return pl.pallas_call(
        _gather_kernel,
        grid_spec=pltpu.PrefetchScalarGridSpec(
            num_scalar_prefetch=1, grid=(n // tm,),
            in_specs=[pl.BlockSpec(memory_space=pl.ANY)],
            out_specs=pl.BlockSpec((tm, d), lambda i, src: (i, 0)),
            scratch_shapes=[pltpu.SemaphoreType.DMA(())]),
        out_shape=jax.ShapeDtypeStruct((n, d), x32.dtype),
        compiler_params=_params("arbitrary"),
        name="moe_gather",
    )(src, x32)


def _combine_kernel(pos_ref, x_ref, wt_ref, ys_hbm, lg_ref, lb_ref, o32_ref, o16_ref, buf0, buf1, sem, *, alpha):
    tm = x_ref.shape[0]
    base = pl.program_id(0) * tm

    def issue(r, par):
        _row_copy(ys_hbm, pos_ref[2 * (base + r)], buf0, r, sem).start(priority=0)
        _row_copy(ys_hbm, pos_ref[2 * (base + r) + 1], buf1, r, sem).start(priority=1)

    def wait(r, par):
        _row_copy(ys_hbm, 0, buf0, r, sem).wait()
        _row_copy(ys_hbm, 0, buf1, r, sem).wait()

    _for_rows(tm, issue)
    _for_rows(tm, wait)
    wt = wt_ref[...]
    ff = wt[:, 0:1] * buf0[...] + wt[:, 1:2] * buf1[...]
    y = _layernorm(alpha * x_ref[...] + ff, lg_ref[...], lb_ref[...])
    o32_ref[...] = y
    o16_ref[...] = y.astype(BF16)


def _combine(pos, x32, wts, ys, lg, lb, *, alpha, tm):
    t, d = x32.shape
    row = lambda i, pos: (i, 0)
    const = lambda i, pos: (0, 0)
    return pl.pallas_call(
        functools.partial(_combine_kernel, alpha=alpha),
        grid_spec=pltpu.PrefetchScalarGridSpec(
            num_scalar_prefetch=1, grid=(t // tm,),
            in_specs=[pl.BlockSpec((tm, d), row), pl.BlockSpec((tm, LANES), row), pl.BlockSpec(memory_space=pl.ANY),
                      pl.BlockSpec((1, d), const), pl.BlockSpec((1, d), const)],
            out_specs=[pl.BlockSpec((tm, d), row)] * 2,
            scratch_shapes=[pltpu.VMEM((tm, d), F32), pltpu.VMEM((tm, d), F32), pltpu.SemaphoreType.DMA(())]),
        out_shape=[jax.ShapeDtypeStruct((t, d), F32), jax.ShapeDtypeStruct((t, d), BF16)],
        compiler_params=_params("arbitrary"),
        name="moe_combine",
    )(pos, x32, wts, ys, lg, lb)


def _route_meta(idx, tm):
    t = idx.shape[0]
    a = 2 * t
    e = idx.reshape(a)
    onehot = (e[:, None] == jnp.arange(N_EXPERTS, dtype=jnp.int32)[None, :]).astype(jnp.int32)
    csum = jnp.cumsum(onehot, axis=0)
    rank = jnp.sum((csum - onehot) * onehot, axis=1)
    cnt = csum[-1]
    pcnt = ((cnt + tm - 1) // tm) * tm
    pend = jnp.cumsum(pcnt)
    pos = jnp.sum(onehot * (pend - pcnt)[None, :], axis=1) + rank
    n_rows = a + N_EXPERTS * tm
    src = jnp.zeros((n_rows,), jnp.int32).at[pos].set(jnp.arange(a, dtype=jnp.int32) // 2, unique_indices=True)
    start = jnp.arange(n_rows // tm, dtype=jnp.int32) * tm
    te = jnp.minimum(jnp.sum((start[:, None] >= pend[None, :]).astype(jnp.int32), axis=1), N_EXPERTS - 1)
    tv = (start < pend[-1]).astype(jnp.int32)
    return pos.astype(jnp.int32), src, te.astype(jnp.int32), tv


def _prep_w_in(w):
    d = w.shape[0]
    z = lambda n: jnp.zeros((d, n), w.dtype)
    na_q, na_k, na_v = w[:, 0:512] * (HEAD_DIM ** -0.5), w[:, 512:1024], w[:, 1024:1536]
    g_q, g_k, g_v = w[:, 1536:2048], w[:, 2048:2176], w[:, 2176:2304]
    c_q, c_kv, k_r, gate = w[:, 2304:2688], w[:, 2688:2944], w[:, 2944:2976], w[:, 2976:]
    kr_blk = jnp.concatenate([z(MLA_NOPE_DIM), k_r, z(LANES - MLA_NOPE_DIM - MLA_ROPE_DIM)], axis=1)
    out = jnp.concatenate([na_q, na_k, na_v, g_q, g_k, g_v, c_kv, kr_blk, c_q, gate], axis=1).astype(BF16)
    assert out.shape[1] == H_COLS
    return out


def _prep_mla_w(w_uq, w_ukv):
    heads = w_uq.shape[1] // (MLA_NOPE_DIM + MLA_ROPE_DIM)
    wq = w_uq.reshape(MLA_Q_LORA, heads, MLA_NOPE_DIM + MLA_ROPE_DIM)
    wq = jnp.pad(wq, ((0, 0), (0, 0), (0, LANES - MLA_NOPE_DIM - MLA_ROPE_DIM))).reshape(MLA_Q_LORA, heads * LANES)
    wkv = w_ukv.reshape(MLA_KV_LORA, heads, LANES)
    wk = jnp.pad(wkv[:, :, :MLA_NOPE_DIM], ((0, 0), (0, 0), (0, LANES - MLA_NOPE_DIM))).reshape(MLA_KV_LORA, heads * LANES)
    wvt = wkv[:, :, MLA_NOPE_DIM:].reshape(MLA_KV_LORA, heads * HEAD_DIM).T
    return wq.astype(BF16), wk.astype(BF16), wvt.astype(BF16)


def _rope_tables(seq):
    pos = jnp.arange(seq, dtype=jnp.int32)
    half = 16
    inv_freq = ROPE_THETA ** (-jnp.arange(half, dtype=F32) / half)

    def cs(p):
        ang = p.astype(F32)[:, None] * inv_freq[None, :]
        return jnp.cos(ang), jnp.sin(ang)

    cr, sr = cs(pos // GRID_W)
    cc, sc = cs(pos % GRID_W)
    cp, sp = cs(pos)
    one = lambda n: jnp.ones((seq, n), F32)
    zero = lambda n: jnp.zeros((seq, n), F32)
    g_cos = jnp.tile(jnp.concatenate([cr, cr, cc, cc], axis=1), (1, 2))
    g_sin = jnp.tile(jnp.concatenate([-sr, sr, -sc, sc], axis=1), (1, 2))
    m_cos = jnp.concatenate([one(MLA_NOPE_DIM), cp, cp, one(32)], axis=1)
    m_sin = jnp.concatenate([zero(MLA_NOPE_DIM), -sp, sp, zero(32)], axis=1)
    return g_cos, g_sin, m_cos, m_sin


def kernel(x_prompt, x_sample, w_in, na_rpb, gqa_q_norm, gqa_k_norm, mla_q_norm, mla_w_uq, mla_kv_norm, mla_w_ukv, w_branch_a, w_branch_b, w_branch_c, w_out, ln1_g, ln1_b, ln2_g, ln2_b, ffn_w1, ffn_w3, ffn_w2, moe_router, moe_w1, moe_w3, moe_w2):
    depth = w_in.shape[0]
    alpha = float((2 * depth) ** 0.25)
    groups = [(x_prompt.shape[0], x_prompt.shape[1]), (x_sample.shape[0], x_sample.shape[1])]
    d = x_prompt.shape[2]
    assert d == D_MODEL
    x32 = jnp.concatenate([x_prompt.reshape(-1, d), x_sample.reshape(-1, d)], axis=0)
    t = x32.shape[0]
    x16 = x32.astype(BF16)

    tabs = [_rope_tables(s) for _, s in groups]
    g_cos, g_sin, m_cos, m_sin = [jnp.concatenate([jnp.tile(tabs[g][n], (groups[g][0], 1)) for g in range(2)], axis=0)
                                  for n in range(4)]
    blk = np.kron(np.eye(2), np.ones((HEAD_DIM, HEAD_DIM)))
    ones_bd = jnp.asarray(blk, BF16)

    tm_proj = _pick(t, 2048)
    tm_tok = _pick(t, 512, LANES)
    tm_moe = _pick(t, 512)
    row2 = lambda v: v.reshape(1, -1).astype(F32)

    for i in range(depth):
        h = _matmul(x16, _prep_w_in(w_in[i]), tm=tm_proj, tn=512)
        qg = row2(jnp.tile(gqa_q_norm[i] * (HEAD_DIM ** -0.5 * LOG2E), 2))
        kg = row2(jnp.tile(gqa_k_norm[i], 2))
        gq, gkd, gvt = _gqa_prep(h, qg, kg, g_cos, g_sin, ones_bd, tm=tm_tok)
        wq, wk, wvt = _prep_mla_w(mla_w_uq[i], mla_w_ukv[i])
        mq, mk, mvt = _mla_prep(h, row2(mla_q_norm[i]), row2(mla_kv_norm[i]), wq, wk, wvt, m_cos, m_sin, tm=tm_tok)
        bias = _na_bias_planes(na_rpb[i])
        ya, ybt, yct = [], [], []
        off = 0
        for batch, seq in groups:
            tq = _pick(seq, 256, LANES)
            ya.append(_na_attn(h, bias, tok_off=off, batch=batch, seq=seq))
            n_sub = seq // SUB_KEYS
            ybt.append(_gqa_attn(gq, gkd, gvt, tok_off=off, batch=batch, seq=seq, tq=tq,
                                 groups_per_step=2 if GQA_GROUP * n_sub < TILES_PER_BODY else 1))
            yct.append(_mla_attn(mq, mk, mvt, tok_off=off, batch=batch, seq=seq, tq=tq,
                                 heads_per_step=min(8, max(2, TILES_PER_BODY // n_sub))))
            off += batch * seq
        x32, x16 = _merge(ya, ybt, yct, h, x32, w_branch_a[i].astype(BF16), w_branch_b[i].astype(BF16),
                          w_branch_c[i].astype(BF16), w_out[i].astype(BF16), row2(ln1_g[i]), row2(ln1_b[i]),
                          alpha=alpha, tm=tm_tok)
        j = i // 2
        if i % 2 == 0:
            f = ffn_w1.shape[2]
            x32, x16 = _ffn_dense(x16, x32, ffn_w1[j:j + 1].astype(BF16), ffn_w3[j:j + 1].astype(BF16),
                                  ffn_w2[j:j + 1].astype(BF16), row2(ln2_g[i]), row2(ln2_b[i]),
                                  alpha=alpha, tm=_pick(t, 256, LANES), tf=_pick(f, 2816, 2 * LANES))
        else:
            rw = jnp.pad(moe_router[j], ((0, 0), (0, LANES - N_EXPERTS)))
            rw_hi = rw.astype(BF16)
            rw_lo = (rw - rw_hi.astype(F32)).astype(BF16)
            idx, wts = _router(x32, rw_hi, rw_lo, tm=tm_tok)
            pos, src, te, tv = _route_meta(idx[:, :2], tm_moe)
            xs = _gather_rows(x32, src, tm=2 * tm_moe)
            f = moe_w1.shape[3]
            ys = _ffn_grouped(xs, te, tv, moe_w1[j].astype(BF16), moe_w3[j].astype(BF16),
                              moe_w2[j].astype(BF16), tm=tm_moe, tf=_pick(f, 1792, 2 * LANES))
            x32, x16 = _combine(pos, x32, wts, ys, row2(ln2_g[i]), row2(ln2_b[i]), alpha=alpha, tm=tm_moe)

    tp = groups[0][0] * groups[0][1]
    return (x32[:tp].reshape(x_prompt.shape), x32[tp:].reshape(x_sample.shape))
```
